```python
import jax, jax.numpy as jnp
from jax import lax
import numpy as np

D_MODEL = 1024
BATCH = 8
SEQ = 2048
DEPTH = 2
DEC_BATCH = 128
DEC_SEQ = 1
PAST_LEN = 16384
PAGE_SIZE = 128

N_META = 16
HEAD_DIM = 64
D_R = D_MODEL
N_HEADS = D_R // HEAD_DIM
D_C = D_MODEL
CONV_W = 3
LORA_W = 64
LORA_A = 64
LORA_G = 128
D_SHIFT = 3 * D_R + LORA_W + LORA_A + LORA_G
D_PROJ = D_SHIFT + 3 * D_C + 2 * D_MODEL
D_FF = 2816
RMS_EPS = 1e-6
GN_EPS = 64e-5

kernel_name = 'rwkv7_shortconv_macaron_hybrid_step'


def _rmsnorm(x, g):
    xf = x.astype(jnp.float32)
    y = xf * lax.rsqrt(jnp.mean(jnp.square(xf), axis=-1, keepdims=True) + RMS_EPS)
    return (y * g.astype(jnp.float32)).astype(x.dtype)


def _swiglu(x, w_in, w_out):
    gate, up = jnp.split(x @ w_in, 2, axis=-1)
    return (jax.nn.silu(gate) * up) @ w_out


def _wkv_step(S, inp):
    r_t, w_t, k_t, v_t, a_t, b_t = inp
    Sa = jnp.einsum('bhvk,bhk->bhv', S, a_t)
    S = S * w_t[:, :, None, :] + Sa[..., None] * b_t[:, :, None, :] + v_t[..., None] * k_t[:, :, None, :]
    y = jnp.einsum('bhvk,bhk->bhv', S, r_t)
    return S, y


def _mixer(xn, wkv0, shift0, conv0, w_in, mu_shift, w0, w_w2, a0, w_a2, w_g2,
           k_k, k_a, r_k, lnx_w, lnx_b, conv_w, w_o):
    f32 = jnp.float32
    bsz, L, _ = xn.shape
    proj = xn @ w_in
    ps, pc, pg = jnp.split(proj, [D_SHIFT, D_SHIFT + 3 * D_C], axis=-1)

    prev = jnp.concatenate([shift0[:, None, :].astype(ps.dtype), ps[:, :-1]], axis=1)
    xs = (ps + (prev - ps) * mu_shift).astype(f32)
    new_shift = ps[:, -1].astype(shift0.dtype)
    r, k, v, lw, la, lg = jnp.split(
        xs, [D_R, 2 * D_R, 3 * D_R, 3 * D_R + LORA_W, 3 * D_R + LORA_W + LORA_A], axis=-1)
    w_log = -jax.nn.softplus(-(w0.astype(f32) + jnp.tanh(lw) @ w_w2.astype(f32))) - 0.5
    decay = jnp.exp(-jnp.exp(w_log))
    a = jax.nn.sigmoid(a0.astype(f32) + la @ w_a2.astype(f32))
    g = jax.nn.sigmoid(lg) @ w_g2.astype(f32)

    def hs(t):
        return t.reshape(bsz, L, N_HEADS, HEAD_DIM)

    kk = hs(k * k_k.astype(f32))
    kk = kk / jnp.maximum(jnp.sqrt(jnp.sum(jnp.square(kk), axis=-1, keepdims=True)), 1e-12)
    kh = hs(k * (1.0 + (a - 1.0) * k_a.astype(f32)))
    ah, rh, vh, wh = hs(a), hs(r), hs(v), hs(decay)
    seq_in = tuple(jnp.moveaxis(t, 1, 0) for t in (rh, wh, kh, vh, -kk, kk * ah))
    S_T, y = lax.scan(_wkv_step, wkv0.astype(f32), seq_in)
    y = jnp.moveaxis(y, 0, 1)
    mu = jnp.mean(y, axis=-1, keepdims=True)
    var = jnp.mean(jnp.square(y - mu), axis=-1, keepdims=True)
    yn = ((y - mu) * lax.rsqrt(var + GN_EPS)).reshape(bsz, L, D_R) * lnx_w.astype(f32) + lnx_b.astype(f32)
    bonus = (jnp.sum(rh * kh * r_k.astype(f32), axis=-1, keepdims=True) * vh).reshape(bsz, L, D_R)
    y_a = ((yn + bonus) * g).astype(xn.dtype)

    gb, gc, hc = jnp.split(pc, 3, axis=-1)
    u = gc * hc
    u_pad = jnp.concatenate([conv0.astype(u.dtype), u], axis=1)
    z = conv_w[0] * u_pad[:, 0:L] + conv_w[1] * u_pad[:, 1:L + 1] + conv_w[2] * u_pad[:, 2:L + 2]
    new_conv = u_pad[:, -(CONV_W - 1):].astype(conv0.dtype)
    y_b = gb * z

    gate_a, gate_b = jnp.split(pg, 2, axis=-1)
    m = jax.nn.sigmoid(gate_a) * y_a + jax.nn.sigmoid(gate_b) * y_b
    return m @ w_o, S_T.astype(wkv0.dtype), new_shift, new_conv


def _trunk(x, wkv_in, shift_in, conv_in, p):
    new_wkv, new_shift, new_conv = [], [], []
    for l in range(DEPTH):
        x = x + 0.5 * _swiglu(_rmsnorm(x, p['norm_ffn1'][l]), p['ffn1_w_in'][l], p['ffn1_w_out'][l])
        h, s_w, s_s, s_c = _mixer(
            _rmsnorm(x, p['norm_mix'][l]), wkv_in[l], shift_in[l], conv_in[l],
            p['w_in'][l], p['mu_shift'][l], p['w0'][l], p['w_w2'][l], p['a0'][l], p['w_a2'][l],
            p['w_g2'][l], p['k_k'][l], p['k_a'][l], p['r_k'][l], p['lnx_w'][l], p['lnx_b'][l],
            p['conv_w'][l], p['w_o'][l])
        x = x + h
        x = x + 0.5 * _swiglu(_rmsnorm(x, p['norm_ffn2'][l]), p['ffn2_w_in'][l], p['ffn2_w_out'][l])
        new_wkv.append(s_w)
        new_shift.append(s_s)
        new_conv.append(s_c)
    y = _rmsnorm(x, p['norm_final'])
    return y, jnp.stack(new_wkv), jnp.stack(new_shift), jnp.stack(new_conv)


def setup_inputs(seed: int = 0) -> dict:
    key = jax.random.key(seed)
    ks = jax.random.split(key, 28)

    def nrm(k, shape, s):
        return jax.random.normal(k, shape, jnp.float32) * s

    return {
        'x_prompt': nrm(ks[0], (BATCH, SEQ, D_MODEL), 1.0),
        'x_sample': nrm(ks[1], (DEC_BATCH, DEC_SEQ, D_MODEL), 1.0),
        'state_wkv': nrm(ks[2], (DEPTH, DEC_BATCH, N_HEADS, HEAD_DIM, HEAD_DIM), 0.3),
        'state_shift': nrm(ks[3], (DEPTH, DEC_BATCH, D_SHIFT), 1.0),
        'state_conv': nrm(ks[4], (DEPTH, DEC_BATCH, CONV_W - 1, D_C), 1.0),
        'meta_tokens': nrm(ks[5], (N_META, D_MODEL), 1.0),
        'norm_ffn1': 1.0 + nrm(ks[6], (DEPTH, D_MODEL), 0.02),
        'ffn1_w_in': nrm(ks[7], (DEPTH, D_MODEL, 2 * D_FF), D_MODEL ** -0.5),
        'ffn1_w_out': nrm(ks[8], (DEPTH, D_FF, D_MODEL), D_FF ** -0.5),
        'norm_mix': 1.0 + nrm(ks[9], (DEPTH, D_MODEL), 0.02),
        'w_in': nrm(ks[10], (DEPTH, D_MODEL, D_PROJ), D_MODEL ** -0.5),
        'mu_shift': jax.random.uniform(ks[11], (DEPTH, D_SHIFT), jnp.float32),
        'w0': jax.random.uniform(ks[12], (DEPTH, D_R), jnp.float32, minval=-4.0, maxval=1.0),
        'w_w2': nrm(ks[13], (DEPTH, LORA_W, D_R), 0.1 * LORA_W ** -0.5),
        'a0': nrm(ks[14], (DEPTH, D_R), 0.1),
        'w_a2': nrm(ks[15], (DEPTH, LORA_A, D_R), 0.5 * LORA_A ** -0.5),
        'w_g2': nrm(ks[16], (DEPTH, LORA_G, D_R), LORA_G ** -0.5),
        'k_k': 0.85 + nrm(ks[17], (DEPTH, D_R), 0.02),
        'k_a': 1.0 + nrm(ks[18], (DEPTH, D_R), 0.02),
        'r_k': nrm(ks[19], (DEPTH, N_HEADS, HEAD_DIM), 0.1),
        'lnx_w': 1.0 + nrm(ks[20], (DEPTH, D_R), 0.02),
        'lnx_b': nrm(ks[21], (DEPTH, D_R), 0.02),
        'conv_w': nrm(ks[22], (DEPTH, CONV_W, D_C), CONV_W ** -0.5),
        'w_o': nrm(ks[23], (DEPTH, D_MODEL, D_MODEL), D_MODEL ** -0.5),
        'norm_ffn2': 1.0 + nrm(ks[24], (DEPTH, D_MODEL), 0.02),
        'ffn2_w_in': nrm(ks[25], (DEPTH, D_MODEL, 2 * D_FF), D_MODEL ** -0.5),
        'ffn2_w_out': nrm(ks[26], (DEPTH, D_FF, D_MODEL), D_FF ** -0.5),
        'norm_final': 1.0 + nrm(ks[27], (D_MODEL,), 0.02),
    }


def reference(x_prompt, x_sample, state_wkv, state_shift, state_conv, meta_tokens,
              norm_ffn1, ffn1_w_in, ffn1_w_out, norm_mix, w_in, mu_shift, w0, w_w2, a0,
              w_a2, w_g2, k_k, k_a, r_k, lnx_w, lnx_b, conv_w, w_o, norm_ffn2,
              ffn2_w_in, ffn2_w_out, norm_final):
    p = dict(norm_ffn1=norm_ffn1, ffn1_w_in=ffn1_w_in, ffn1_w_out=ffn1_w_out,
             norm_mix=norm_mix, w_in=w_in, mu_shift=mu_shift, w0=w0, w_w2=w_w2, a0=a0,
             w_a2=w_a2, w_g2=w_g2, k_k=k_k, k_a=k_a, r_k=r_k, lnx_w=lnx_w, lnx_b=lnx_b,
             conv_w=conv_w, w_o=w_o, norm_ffn2=norm_ffn2, ffn2_w_in=ffn2_w_in,
             ffn2_w_out=ffn2_w_out, norm_final=norm_final)

    bp = x_prompt.shape[0]
    dt = x_prompt.dtype
    meta = jnp.broadcast_to(meta_tokens.astype(dt)[None], (bp, N_META, D_MODEL))
    xp = jnp.concatenate([meta, x_prompt], axis=1)
    wkv0 = jnp.zeros((DEPTH, bp, N_HEADS, HEAD_DIM, HEAD_DIM), dt)
    shift0 = jnp.zeros((DEPTH, bp, D_SHIFT), dt)
    conv0 = jnp.zeros((DEPTH, bp, CONV_W - 1, D_C), dt)
    yp, p_wkv, p_shift, p_conv = _trunk(xp, wkv0, shift0, conv0, p)
    y_prompt = yp[:, N_META:]

    y_sample, s_wkv, s_shift, s_conv = _trunk(x_sample, state_wkv, state_shift, state_conv, p)

    return (y_prompt, y_sample, p_wkv, p_shift, p_conv, s_wkv, s_shift, s_conv)
```

```python
import functools
import math

import jax
import jax.numpy as jnp
from jax import lax
from jax.experimental import pallas as pl
from jax.experimental.pallas import tpu as pltpu

F32 = jnp.float32
BF16 = jnp.bfloat16

D = 1024
H = 16
HD = 64
G = 4
GL = G * HD
NG = H // G
LORA_WA = 128
LORA_G = 128
DS = 3 * D + LORA_WA + LORA_G
DP = DS + 3 * D + 2 * D
DFF = 2816
N_META = 16
RMS_EPS = 1e-6
GN_EPS = 64e-5
EXP_M05 = math.exp(-0.5)

CHUNK = 64
SEQ_TILE = 256
ROW_TILE = 512
STEP_TILE = 16
FF_SPLIT = 2
PROJ_COLS = 2816
VMEM_LIMIT = 56 * 1024 * 1024

_NT = (((1,), (1,)), ((), ()))
_TN = (((0,), (0,)), ((), ()))


def _dot(a, b):
    return jnp.dot(a, b, preferred_element_type=F32)


def _rmsnorm(x, g):
    return x * lax.rsqrt(jnp.mean(x * x, axis=-1, keepdims=True) + RMS_EPS) * g


def _sigmoid(x):
    return 1.0 / (1.0 + jnp.exp(-x))


def _const_spec(shape):
    nd = len(shape)
    return pl.BlockSpec(shape, lambda *_: (0,) * nd)


def _ffn_body(x_ref, g_ref, win_ref, wout_ref, gf_ref, o_ref, *, final_norm):
    x = x_ref[...]
    xb = _rmsnorm(x, g_ref[...]).astype(BF16)
    piece = DFF // FF_SPLIT
    acc = None
    for c in range(FF_SPLIT):
        gate = _dot(xb, win_ref[:, c * piece:(c + 1) * piece])
        up = _dot(xb, win_ref[:, DFF + c * piece:DFF + (c + 1) * piece])
        act = (gate * _sigmoid(gate) * up).astype(BF16)
        part = _dot(act, wout_ref[c * piece:(c + 1) * piece, :])
        acc = part if acc is None else acc + part
    y = x + 0.5 * acc
    if final_norm:
        y = _rmsnorm(y, gf_ref[...])
    o_ref[...] = y


def _ffn(x, g, w_in, w_out, g_final, final_norm):
    rows = x.shape[0]
    tm = min(ROW_TILE, rows)
    assert rows % tm == 0
    return pl.pallas_call(
        functools.partial(_ffn_body, final_norm=final_norm),
        grid=(rows // tm,),
        in_specs=[
            pl.BlockSpec((tm, D), lambda i: (i, 0)),
            _const_spec((1, D)),
            pl.BlockSpec((D, 2 * DFF), lambda i: (0, 0), pipeline_mode=pl.Buffered(1)),
            pl.BlockSpec((DFF, D), lambda i: (0, 0), pipeline_mode=pl.Buffered(1)),
            _const_spec((1, D)),
        ],
        out_specs=pl.BlockSpec((tm, D), lambda i: (i, 0)),
        out_shape=jax.ShapeDtypeStruct((rows, D), F32),
        compiler_params=pltpu.CompilerParams(
            dimension_semantics=("arbitrary",), vmem_limit_bytes=VMEM_LIMIT),
        name="ffn",
    )(x, g, w_in, w_out, g_final)


def _proj_body(x_ref, g_ref, w_ref, o_ref):
    xb = _rmsnorm(x_ref[...], g_ref[...]).astype(BF16)
    o_ref[...] = _dot(xb, w_ref[...])


def _proj(x, g, w):
    rows = x.shape[0]
    tm = min(ROW_TILE, rows)
    assert rows % tm == 0 and DP % PROJ_COLS == 0
    return pl.pallas_call(
        _proj_body,
        grid=(DP // PROJ_COLS, rows // tm),
        in_specs=[
            pl.BlockSpec((tm, D), lambda n, i: (i, 0)),
            _const_spec((1, D)),
            pl.BlockSpec((D, PROJ_COLS), lambda n, i: (0, n)),
        ],
        out_specs=pl.BlockSpec((tm, PROJ_COLS), lambda n, i: (i, n)),
        out_shape=jax.ShapeDtypeStruct((rows, DP), F32),
        compiler_params=pltpu.CompilerParams(
            dimension_semantics=("arbitrary", "arbitrary"), vmem_limit_bytes=VMEM_LIMIT),
        name="proj",
    )(x, g, w)


def _seg_sum(x, e):
    outs = []
    for g in range(NG):
        xg = x[:, g * GL:(g + 1) * GL]
        hi = xg.astype(BF16)
        lo = (xg - hi.astype(F32)).astype(BF16)
        outs.append(_dot(hi, e) + _dot(lo, e))
    return jnp.concatenate(outs, axis=1)


def _mix_heads(cur, prev, mu_ref, w0_ref, w2_ref, a0_ref, a2_ref, wg2_ref, kk_ref, ka_ref, rk_ref, e):
    def xs(a, b):
        c = cur(a, b)
        return c + (prev(a, b) - c) * mu_ref[:, a:b]

    r = xs(0, D)
    k = xs(D, 2 * D)
    v = xs(2 * D, 3 * D)
    lwla = xs(3 * D, 3 * D + LORA_WA)
    lg = xs(3 * D + LORA_WA, DS)
    wl = w0_ref[...] + _dot(jnp.tanh(lwla).astype(BF16), w2_ref[...])
    dexp = EXP_M05 * _sigmoid(wl)
    alr = _sigmoid(a0_ref[...] + _dot(lwla.astype(BF16), a2_ref[...]))
    gate = _dot(_sigmoid(lg).astype(BF16), wg2_ref[...])
    kkr = k * kk_ref[...]
    kk = kkr / jnp.maximum(jnp.sqrt(_seg_sum(kkr * kkr, e)), 1e-12)
    kh = k * (1.0 + (alr - 1.0) * ka_ref[...])
    bonus = _seg_sum(r * kh * rk_ref[...], e) * v
    return r, kh, v, kk, alr, dexp, gate, bonus


def _mix_out(y, bonus, gate, gbc, z, ga, gb, x, lnw_ref, lnb_ref, wo_ref, e):
    mu = _seg_sum(y, e) * (1.0 / HD)
    d = y - mu
    var = _seg_sum(d * d, e) * (1.0 / HD)
    yn = d * lax.rsqrt(var + GN_EPS) * lnw_ref[...] + lnb_ref[...]
    ya = (yn + bonus) * gate
    m = _sigmoid(ga) * ya + _sigmoid(gb) * (gbc * z)
    return x + _dot(m.astype(BF16), wo_ref[...])


def _mix_seq_body(proj_ref, x_ref, s0_ref, sh0_ref, cv0_ref,
                  mu_ref, w0_ref, w2_ref, a0_ref, a2_ref, wg2_ref, kk_ref, ka_ref, rk_ref,
                  lnw_ref, lnb_ref, cw_ref, wo_ref, e_ref,
                  o_ref, st_ref, sh_ref, cv_ref,
                  ah_s, rh_s, bt_s, kt_s, v_s, wl_s, y_s, *, tl, c):
    j = pl.program_id(1)
    gc = G * c
    nchunk = tl // c

    @pl.when(j == 0)
    def _():
        st_ref[...] = s0_ref[...]
        sh_ref[...] = sh0_ref[...]
        cv_ref[...] = cv0_ref[...]

    e = e_ref[...]
    row = lax.broadcasted_iota(jnp.int32, (tl, 1), 0)

    def cur(a, b):
        return proj_ref[0, :, a:b]

    def prev(a, b):
        return jnp.where(row == 0, sh_ref[0, :, a:b], pltpu.roll(cur(a, b), 1, 0))

    r, kh, v, kk, alr, dexp, gate, bonus = _mix_heads(
        cur, prev, mu_ref, w0_ref, w2_ref, a0_ref, a2_ref, wg2_ref, kk_ref, ka_ref, rk_ref, e)
    new_shift = proj_ref[0, tl - 1:tl, 0:DS]

    rin = row % c
    cum = dexp
    s = 1
    while s < c:
        cum = cum + jnp.where(rin >= s, pltpu.roll(cum, s, 0), 0.0)
        s *= 2
    wc = jnp.exp(-cum)
    winv = jnp.exp(cum)
    rh_s[...] = (r * wc).astype(BF16)
    ah_s[...] = (-kk * jnp.exp(dexp - cum)).astype(BF16)
    bt_s[...] = (kk * alr * winv).astype(BF16)
    kt_s[...] = (kh * winv).astype(BF16)
    v_s[...] = v.astype(BF16)
    for ci in range(nchunk):
        wl_s[ci:ci + 1, :] = wc[(ci + 1) * c - 1:(ci + 1) * c, :]

    ri = lax.broadcasted_iota(jnp.int32, (gc, gc), 0)
    cj = lax.broadcasted_iota(jnp.int32, (gc, gc), 1)
    same = (ri // c) == (cj // c)
    strict = same & ((ri % c) > (cj % c))
    incl = same & ((ri % c) >= (cj % c))
    eye = (ri == cj).astype(F32)
    smask = (lax.broadcasted_iota(jnp.int32, (gc, GL), 0) // c) == (
        lax.broadcasted_iota(jnp.int32, (gc, GL), 1) // HD)

    def stack(x):
        return jnp.where(smask, jnp.concatenate([x] * G, axis=0), jnp.zeros((), x.dtype))

    def chunk_step(ci, carry):
        rs = pl.ds(pl.multiple_of(ci * c, c), c)
        for g in range(NG):
            ls = slice(g * GL, (g + 1) * GL)
            ah, rh, bt, kt, vv = ah_s[rs, ls], rh_s[rs, ls], bt_s[rs, ls], kt_s[rs, ls], v_s[rs, ls]
            a_st, r_st, b_st, k_st, v_st = stack(ah), stack(rh), stack(bt), stack(kt), stack(vv)
            b_rep = jnp.concatenate([bt] * G, axis=0)
            k_rep = jnp.concatenate([kt] * G, axis=0)
            l_ab = jnp.where(strict, lax.dot_general(a_st, b_rep, _NT, preferred_element_type=F32), 0.0)
            a_ak = jnp.where(strict, lax.dot_general(a_st, k_rep, _NT, preferred_element_type=F32), 0.0)
            a_rb = jnp.where(incl, lax.dot_general(r_st, b_rep, _NT, preferred_element_type=F32), 0.0)
            a_rk = jnp.where(incl, lax.dot_general(r_st, k_rep, _NT, preferred_element_type=F32), 0.0)
            p = eye + l_ab
            lp = l_ab.astype(BF16)
            n = 1
            while 2 * n < c:
                lp = _dot(lp, lp).astype(BF16)
                p = p + _dot(p.astype(BF16), lp)
                n *= 2
            st = st_ref[0, g]
            ars = lax.dot_general(jnp.concatenate([ah, rh], axis=0), st.astype(BF16), _NT,
                                  preferred_element_type=F32)
            x_st = stack(ars[:c]) + _dot(a_ak.astype(BF16), v_st)
            u_st = _dot(p.astype(BF16), x_st.astype(BF16)).astype(BF16)
            y_st = _dot(a_rb.astype(BF16), u_st) + _dot(a_rk.astype(BF16), v_st)
            y = ars[c:]
            for hh in range(G):
                y = y + y_st[hh * c:(hh + 1) * c]
            y_s[rs, ls] = y
            upd = (lax.dot_general(u_st, b_st, _TN, preferred_element_type=F32)
                   + lax.dot_general(v_st, k_st, _TN, preferred_element_type=F32))
            st_ref[0, g] = (st + upd) * wl_s[pl.ds(ci, 1), ls]
        return carry

    lax.fori_loop(0, nchunk, chunk_step, 0)

    u = proj_ref[0, :, DS + D:DS + 2 * D] * proj_ref[0, :, DS + 2 * D:DS + 3 * D]
    c0 = cv_ref[0, 0:1, :]
    c1 = cv_ref[0, 1:2, :]
    u1 = jnp.where(row == 0, c1, pltpu.roll(u, 1, 0))
    u2 = jnp.where(row == 0, c0, jnp.where(row == 1, c1, pltpu.roll(u, 2, 0)))
    z = cw_ref[0:1, :] * u2 + cw_ref[1:2, :] * u1 + cw_ref[2:3, :] * u
    o_ref[0] = _mix_out(y_s[...], bonus, gate, proj_ref[0, :, DS:DS + D], z,
                        proj_ref[0, :, DS + 3 * D:DS + 4 * D], proj_ref[0, :, DS + 4 * D:DP],
                        x_ref[0], lnw_ref, lnb_ref, wo_ref, e)
    sh_ref[0] = new_shift
    cv_ref[0] = u[tl - 2:tl, :]


def _mix_seq(proj, x, s0, sh0, cv0, mp, *, tl, c, shared_state):
    bsz, length, _ = x.shape
    assert length % tl == 0 and tl % c == 0 and tl >= 2
    nchunk = tl // c

    def st_map(b, j):
        return (0 if shared_state else b, 0, 0, 0)

    def vec_map(b, j):
        return (0 if shared_state else b, 0, 0)

    weights = (mp["mu"], mp["w0"], mp["w2"], mp["a0"], mp["a2"], mp["wg2"], mp["k_k"], mp["k_a"],
               mp["r_k"], mp["lnw"], mp["lnb"], mp["cw"], mp["wo"], mp["e"])
    return pl.pallas_call(
        functools.partial(_mix_seq_body, tl=tl, c=c),
        grid=(bsz, length // tl),
        in_specs=[
            pl.BlockSpec((1, tl, DP), lambda b, j: (b, j, 0)),
            pl.BlockSpec((1, tl, D), lambda b, j: (b, j, 0)),
            pl.BlockSpec((1, NG, GL, GL), st_map),
            pl.BlockSpec((1, 1, DS), vec_map),
            pl.BlockSpec((1, 2, D), vec_map),
        ] + [_const_spec(w.shape) for w in weights],
        out_specs=[
            pl.BlockSpec((1, tl, D), lambda b, j: (b, j, 0)),
            pl.BlockSpec((1, NG, GL, GL), lambda b, j: (b, 0, 0, 0)),
            pl.BlockSpec((1, 1, DS), lambda b, j: (b, 0, 0)),
            pl.BlockSpec((1, 2, D), lambda b, j: (b, 0, 0)),
        ],
        out_shape=[
            jax.ShapeDtypeStruct((bsz, length, D), F32),
            jax.ShapeDtypeStruct((bsz, NG, GL, GL), F32),
            jax.ShapeDtypeStruct((bsz, 1, DS), F32),
            jax.ShapeDtypeStruct((bsz, 2, D), F32),
        ],
        scratch_shapes=[pltpu.VMEM((tl, D), BF16)] * 5 + [
            pltpu.VMEM((max(8, nchunk), D), F32),
            pltpu.VMEM((tl, D), F32),
        ],
        compiler_params=pltpu.CompilerParams(
            dimension_semantics=("arbitrary", "arbitrary"), vmem_limit_bytes=VMEM_LIMIT),
        name=f"mix_seq_{tl}",
    )(proj, x, s0, sh0, cv0, *weights)


def _mix_step_body(proj_ref, x_ref, s_ref, sh_ref, cv_ref,
                   mu_ref, w0_ref, w2_ref, a0_ref, a2_ref, wg2_ref, kk_ref, ka_ref, rk_ref,
                   lnw_ref, lnb_ref, cw_ref, wo_ref, e_ref,
                   o_ref, so_ref, sho_ref, cvo_ref,
                   r_s, w_s, k_s, v_s, a_s, b_s, y_s, *, bt):
    e = e_ref[...]

    def cur(a, b):
        return proj_ref[:, a:b]

    def prev(a, b):
        return sh_ref[:, a:b]

    r, kh, v, kk, alr, dexp, gate, bonus = _mix_heads(
        cur, prev, mu_ref, w0_ref, w2_ref, a0_ref, a2_ref, wg2_ref, kk_ref, ka_ref, rk_ref, e)
    r_s[...] = r
    w_s[...] = jnp.exp(-dexp)
    k_s[...] = kh
    v_s[...] = v
    a_s[...] = -kk
    b_s[...] = kk * alr
    eye = (lax.broadcasted_iota(jnp.int32, (HD, HD), 0)
           == lax.broadcasted_iota(jnp.int32, (HD, HD), 1)).astype(F32)

    for b in range(bt):
        rb = slice(b, b + 1)
        for h in range(H):
            ls = slice(h * HD, (h + 1) * HD)
            s = s_ref[b, h]
            sa = jnp.sum(s * a_s[rb, ls], axis=1, keepdims=True)
            v_col = jnp.sum(eye * v_s[rb, ls], axis=1, keepdims=True)
            sn = s * w_s[rb, ls] + sa * b_s[rb, ls] + v_col * k_s[rb, ls]
            so_ref[b, h] = sn
            y_col = jnp.sum(sn * r_s[rb, ls], axis=1, keepdims=True)
            y_s[rb, ls] = jnp.sum(eye * y_col, axis=0, keepdims=True)

    u = proj_ref[:, DS + D:DS + 2 * D] * proj_ref[:, DS + 2 * D:DS + 3 * D]
    c0 = cv_ref[:, 0, :]
    c1 = cv_ref[:, 1, :]
    z = cw_ref[0:1, :] * c0 + cw_ref[1:2, :] * c1 + cw_ref[2:3, :] * u
    o_ref[...] = _mix_out(y_s[...], bonus, gate, proj_ref[:, DS:DS + D], z,
                          proj_ref[:, DS + 3 * D:DS + 4 * D], proj_ref[:, DS + 4 * D:DP],
                          x_ref[...], lnw_ref, lnb_ref, wo_ref, e)
    sho_ref[...] = proj_ref[:, 0:DS]
    cvo_ref[:, 0, :] = c1
    cvo_ref[:, 1, :] = u


def _mix_step(proj, x, s, sh, cv, mp):
    bsz = x.shape[0]
    bt = STEP_TILE
    assert bsz % bt == 0
    weights = (mp["mu"], mp["w0"], mp["w2"], mp["a0"], mp["a2"], mp["wg2"], mp["k_k"], mp["k_a"],
               mp["r_k"], mp["lnw"], mp["lnb"], mp["cw"], mp["wo"], mp["e"])
    return pl.pallas_call(
        functools.partial(_mix_step_body, bt=bt),
        grid=(bsz // bt,),
        in_specs=[
            pl.BlockSpec((bt, DP), lambda i: (i, 0)),
            pl.BlockSpec((bt, D), lambda i: (i, 0)),
            pl.BlockSpec((bt, H, HD, HD), lambda i: (i, 0, 0, 0)),
            pl.BlockSpec((bt, DS), lambda i: (i, 0)),
            pl.BlockSpec((bt, 2, D), lambda i: (i, 0, 0)),
        ] + [_const_spec(w.shape) for w in weights],
        out_specs=[
            pl.BlockSpec((bt, D), lambda i: (i, 0)),
            pl.BlockSpec((bt, H, HD, HD), lambda i: (i, 0, 0, 0)),
            pl.BlockSpec((bt, DS), lambda i: (i, 0)),
            pl.BlockSpec((bt, 2, D), lambda i: (i, 0, 0)),
        ],
        out_shape=[
            jax.ShapeDtypeStruct((bsz, D), F32),
            jax.ShapeDtypeStruct((bsz, H, HD, HD), F32),
            jax.ShapeDtypeStruct((bsz, DS), F32),
            jax.ShapeDtypeStruct((bsz, 2, D), F32),
        ],
        scratch_shapes=[pltpu.VMEM((bt, D), F32)] * 7,
        compiler_params=pltpu.CompilerParams(
            dimension_semantics=("arbitrary",), vmem_limit_bytes=VMEM_LIMIT),
        name="mix_step",
    )(proj, x, s, sh, cv, *weights)


def _to_block_diag(s):
    bsz = s.shape[0]
    s5 = s.reshape(bsz, NG, G, HD, HD)
    eye = jnp.eye(G, dtype=s.dtype)
    bd = s5[:, :, :, :, None, :] * eye[None, None, :, None, :, None]
    return bd.reshape(bsz, NG, GL, GL)


def _from_block_diag(bd):
    bsz = bd.shape[0]
    b6 = bd.reshape(bsz, NG, G, HD, G, HD)
    eye = jnp.eye(G, dtype=bd.dtype)
    return jnp.einsum("bnghjk,gj->bnghk", b6, eye).reshape(bsz, H, HD, HD)


def _layer_params(l, norm_ffn1, ffn1_w_in, ffn1_w_out, norm_mix, w_in, mu_shift, w0, w_w2, a0,
                  w_a2, w_g2, k_k, k_a, r_k, lnx_w, lnx_b, conv_w, w_o, norm_ffn2, ffn2_w_in,
                  ffn2_w_out):
    zeros = jnp.zeros((LORA_WA // 2, D), F32)
    lane = jnp.arange(GL)
    return dict(
        n1=norm_ffn1[l][None], f1_in=ffn1_w_in[l].astype(BF16), f1_out=ffn1_w_out[l].astype(BF16),
        nm=norm_mix[l][None], w_in=w_in[l].astype(BF16),
        n2=norm_ffn2[l][None], f2_in=ffn2_w_in[l].astype(BF16), f2_out=ffn2_w_out[l].astype(BF16),
        mix=dict(
            mu=mu_shift[l][None], w0=w0[l][None],
            w2=jnp.concatenate([w_w2[l], zeros], axis=0).astype(BF16),
            a0=a0[l][None],
            a2=jnp.concatenate([zeros, w_a2[l]], axis=0).astype(BF16),
            wg2=w_g2[l].astype(BF16), k_k=k_k[l][None], k_a=k_a[l][None],
            r_k=r_k[l].reshape(1, D), lnw=lnx_w[l][None], lnb=lnx_b[l][None],
            cw=conv_w[l], wo=w_o[l].astype(BF16),
            e=(lane[:, None] // HD == lane[None, :] // HD).astype(BF16),
        ),
    )


def _trunk(x_meta, x_prompt, x_sample, state_wkv, state_shift, state_conv, layers, norm_final):
    bp, lp, _ = x_prompt.shape
    depth = len(layers)
    gfin = norm_final[None]
    xm = x_meta.reshape(-1, D)
    xp = x_prompt.reshape(-1, D)
    xs = x_sample
    zero_bd = jnp.zeros((1, NG, GL, GL), F32)
    zero_sh = jnp.zeros((1, 1, DS), F32)
    zero_cv = jnp.zeros((1, 2, D), F32)
    p_wkv, p_shift, p_conv, s_wkv, s_shift, s_conv = [], [], [], [], [], []
    for l, p in enumerate(layers):
        xm = _ffn(xm, p["n1"], p["f1_in"], p["f1_out"], gfin, False)
        xp = _ffn(xp, p["n1"], p["f1_in"], p["f1_out"], gfin, False)
        xs = _ffn(xs, p["n1"], p["f1_in"], p["f1_out"], gfin, False)
        pm = _proj(xm, p["nm"], p["w_in"])
        pp = _proj(xp, p["nm"], p["w_in"])
        psm = _proj(xs, p["nm"], p["w_in"])
        nm = xm.shape[0]
        xm3, m_bd, m_sh, m_cv = _mix_seq(pm.reshape(1, nm, DP), xm.reshape(1, nm, D), zero_bd, zero_sh,
                                         zero_cv, p["mix"], tl=nm, c=nm, shared_state=True)
        xp3, p_bd, p_sh, p_cv = _mix_seq(pp.reshape(bp, lp, DP), xp.reshape(bp, lp, D), m_bd, m_sh, m_cv,
                                         p["mix"], tl=min(SEQ_TILE, lp), c=CHUNK, shared_state=True)
        xs, s_w, s_sh, s_cv = _mix_step(psm, xs, state_wkv[l], state_shift[l], state_conv[l], p["mix"])
        xm = xm3.reshape(-1, D)
        xp = xp3.reshape(-1, D)
        last = l == depth - 1
        xm = _ffn(xm, p["n2"], p["f2_in"], p["f2_out"], gfin, last)
        xp = _ffn(xp, p["n2"], p["f2_in"], p["f2_out"], gfin, last)
        xs = _ffn(xs, p["n2"], p["f2_in"], p["f2_out"], gfin, last)
        p_wkv.append(_from_block_diag(p_bd))
        p_shift.append(p_sh[:, 0])
        p_conv.append(p_cv)
        s_wkv.append(s_w)
        s_shift.append(s_sh)
        s_conv.append(s_cv)
    return (xp.reshape(bp, lp, D), xs, jnp.stack(p_wkv), jnp.stack(p_shift), jnp.stack(p_conv),
            jnp.stack(s_wkv), jnp.stack(s_shift), jnp.stack(s_conv))


def kernel(x_prompt, x_sample, state_wkv, state_shift, state_conv, meta_tokens, norm_ffn1, ffn1_w_in, ffn1_w_out, norm_mix, w_in, mu_shift, w0, w_w2, a0, w_a2, w_g2, k_k, k_a, r_k, lnx_w, lnx_b, conv_w, w_o, norm_ffn2, ffn2_w_in, ffn2_w_out, norm_final):
    depth = norm_ffn1.shape[0]
    layers = [_layer_params(l, norm_ffn1, ffn1_w_in, ffn1_w_out, norm_mix, w_in, mu_shift, w0, w_w2, a0,
                            w_a2, w_g2, k_k, k_a, r_k, lnx_w, lnx_b, conv_w, w_o, norm_ffn2,
                            ffn2_w_in, ffn2_w_out) for l in range(depth)]
    bs = x_sample.shape[0]
    y_prompt, y_sample, p_wkv, p_shift, p_conv, s_wkv, s_shift, s_conv = _trunk(
        meta_tokens[None].astype(x_prompt.dtype), x_prompt, x_sample.reshape(bs, D),
        state_wkv, state_shift, state_conv, layers, norm_final)
    return (y_prompt, y_sample.reshape(bs, 1, D), p_wkv, p_shift, p_conv, s_wkv, s_shift, s_conv)
```

```python
import functools
import math

import jax
import jax.numpy as jnp
from jax import lax
from jax.experimental import pallas as pl
from jax.experimental.pallas import tpu as pltpu

F32 = jnp.float32
BF16 = jnp.bfloat16

D = 1024
H = 16
HD = 64
G = 4
GL = G * HD
NG = H // G
LORA_WA = 128
LORA_G = 128
DS = 3 * D + LORA_WA + LORA_G
DP = DS + 3 * D + 2 * D
DFF = 2816
RMS_EPS = 1e-6
GN_EPS = 64e-5
EXP_M05 = math.exp(-0.5)
LANES = 128
PK = HD * HD // LANES

CHUNK = 64
SEQ_TILE = 256
ROW_TILE = 512
STEP_TILE = 16
FF_SPLIT = 2
PROJ_COLS = 2816
VMEM_LIMIT = 56 * 1024 * 1024

_NT = (((1,), (1,)), ((), ()))
_TN = (((0,), (0,)), ((), ()))

_MIX_WEIGHTS = ("mu", "w0", "w2", "a0", "a2", "wg2", "k_k", "k_a", "r_k", "lnw", "lnb", "cw", "wo")


def _dot(a, b):
    return jnp.dot(a, b, preferred_element_type=F32)


def _dot_split(x, m):
    hi = x.astype(BF16)
    lo = (x - hi.astype(F32)).astype(BF16)
    return _dot(hi, m) + _dot(lo, m)


def _rmsnorm(x, g):
    return x * lax.rsqrt(jnp.mean(x * x, axis=-1, keepdims=True) + RMS_EPS) * g


def _sigmoid(x):
    return 1.0 / (1.0 + jnp.exp(-x))


def _const_spec(shape):
    nd = len(shape)
    return pl.BlockSpec(shape, lambda *_: (0,) * nd)


def _layer_spec(arr, l, **kw):
    nd = arr.ndim - 1
    return pl.BlockSpec((None,) + arr.shape[1:], lambda *_: (l,) + (0,) * nd, **kw)


def _ffn_body(x_ref, g_ref, win_ref, wout_ref, gf_ref, o_ref, *, final_norm):
    x = x_ref[...]
    xb = _rmsnorm(x, g_ref[...]).astype(BF16)
    piece = DFF // FF_SPLIT
    acc = None
    for c in range(FF_SPLIT):
        gate = _dot(xb, win_ref[:, c * piece:(c + 1) * piece])
        up = _dot(xb, win_ref[:, DFF + c * piece:DFF + (c + 1) * piece])
        act = (gate * _sigmoid(gate) * up).astype(BF16)
        part = _dot(act, wout_ref[c * piece:(c + 1) * piece, :])
        acc = part if acc is None else acc + part
    y = x + 0.5 * acc
    if final_norm:
        y = _rmsnorm(y, gf_ref[...])
    o_ref[...] = y


def _ffn(x, l, g, w_in, w_out, g_final, final_norm):
    rows = x.shape[0]
    tm = min(ROW_TILE, rows)
    assert rows % tm == 0
    return pl.pallas_call(
        functools.partial(_ffn_body, final_norm=final_norm),
        grid=(rows // tm,),
        in_specs=[
            pl.BlockSpec((tm, D), lambda i: (i, 0)),
            _layer_spec(g, l),
            _layer_spec(w_in, l, pipeline_mode=pl.Buffered(1)),
            _layer_spec(w_out, l, pipeline_mode=pl.Buffered(1)),
            _const_spec((1, D)),
        ],
        out_specs=pl.BlockSpec((tm, D), lambda i: (i, 0)),
        out_shape=jax.ShapeDtypeStruct((rows, D), F32),
        compiler_params=pltpu.CompilerParams(
            dimension_semantics=("arbitrary",), vmem_limit_bytes=VMEM_LIMIT),
        name="ffn",
    )(x, g, w_in, w_out, g_final)


def _proj_body(x_ref, g_ref, w_ref, o_ref):
    xb = _rmsnorm(x_ref[...], g_ref[...]).astype(BF16)
    o_ref[...] = _dot(xb, w_ref[...])


def _proj(x, l, g, w):
    rows = x.shape[0]
    tm = min(ROW_TILE, rows)
    assert rows % tm == 0 and DP % PROJ_COLS == 0
    return pl.pallas_call(
        _proj_body,
        grid=(DP // PROJ_COLS, rows // tm),
        in_specs=[
            pl.BlockSpec((tm, D), lambda n, i: (i, 0)),
            _layer_spec(g, l),
            pl.BlockSpec((None, D, PROJ_COLS), lambda n, i: (l, 0, n)),
        ],
        out_specs=pl.BlockSpec((tm, PROJ_COLS), lambda n, i: (i, n)),
        out_shape=jax.ShapeDtypeStruct((rows, DP), F32),
        compiler_params=pltpu.CompilerParams(
            dimension_semantics=("arbitrary", "arbitrary"), vmem_limit_bytes=VMEM_LIMIT),
        name="proj",
    )(x, g, w)


def _seg_sum(x, e):
    return jnp.concatenate([_dot_split(x[:, g * GL:(g + 1) * GL], e) for g in range(NG)], axis=1)


def _mix_heads(cur, prev, mu_ref, w0_ref, w2_ref, a0_ref, a2_ref, wg2_ref, kk_ref, ka_ref, rk_ref, e):
    def xs(a, b):
        c = cur(a, b)
        return c + (prev(a, b) - c) * mu_ref[:, a:b]

    r = xs(0, D)
    k = xs(D, 2 * D)
    v = xs(2 * D, 3 * D)
    lwla = xs(3 * D, 3 * D + LORA_WA)
    lg = xs(3 * D + LORA_WA, DS)
    wl = w0_ref[...] + _dot(jnp.tanh(lwla).astype(BF16), w2_ref[...])
    dexp = EXP_M05 * _sigmoid(wl)
    alr = _sigmoid(a0_ref[...] + _dot(lwla.astype(BF16), a2_ref[...]))
    gate = _dot(_sigmoid(lg).astype(BF16), wg2_ref[...])
    kkr = k * kk_ref[...]
    kk = kkr / jnp.maximum(jnp.sqrt(_seg_sum(kkr * kkr, e)), 1e-12)
    kh = k * (1.0 + (alr - 1.0) * ka_ref[...])
    bonus = _seg_sum(r * kh * rk_ref[...], e) * v
    return r, kh, v, kk, alr, dexp, gate, bonus


def _mix_out(y, bonus, gate, gbc, z, ga, gb, x, lnw_ref, lnb_ref, wo_ref, e):
    mu = _seg_sum(y, e) * (1.0 / HD)
    d = y - mu
    var = _seg_sum(d * d, e) * (1.0 / HD)
    yn = d * lax.rsqrt(var + GN_EPS) * lnw_ref[...] + lnb_ref[...]
    ya = (yn + bonus) * gate
    m = _sigmoid(ga) * ya + _sigmoid(gb) * (gbc * z)
    return x + _dot(m.astype(BF16), wo_ref[...])


def _mix_seq_body(proj_ref, x_ref, s0_ref, sh0_ref, cv0_ref,
                  mu_ref, w0_ref, w2_ref, a0_ref, a2_ref, wg2_ref, kk_ref, ka_ref, rk_ref,
                  lnw_ref, lnb_ref, cw_ref, wo_ref, e_ref,
                  o_ref, so_ref, sh_ref, cv_ref,
                  st_s, ah_s, rh_s, bt_s, kt_s, v_s, wl_s, y_s, *, tl, c):
    j = pl.program_id(1)
    gc = G * c
    nchunk = tl // c

    @pl.when(j == 0)
    def _():
        st_s[...] = jnp.zeros_like(st_s)
        for h in range(H):
            d0 = (h % G) * HD
            st_s[h // G, d0:d0 + HD, d0:d0 + HD] = s0_ref[0, h]
        sh_ref[...] = sh0_ref[...]
        cv_ref[...] = cv0_ref[...]

    e = e_ref[...]
    row = lax.broadcasted_iota(jnp.int32, (tl, 1), 0)

    def cur(a, b):
        return proj_ref[0, :, a:b]

    def prev(a, b):
        return jnp.where(row == 0, sh_ref[0, :, a:b], pltpu.roll(cur(a, b), 1, 0))

    r, kh, v, kk, alr, dexp, gate, bonus = _mix_heads(
        cur, prev, mu_ref, w0_ref, w2_ref, a0_ref, a2_ref, wg2_ref, kk_ref, ka_ref, rk_ref, e)
    new_shift = proj_ref[0, tl - 1:tl, 0:DS]

    rin = row % c
    cum = dexp
    s = 1
    while s < c:
        cum = cum + jnp.where(rin >= s, pltpu.roll(cum, s, 0), 0.0)
        s *= 2
    wc = jnp.exp(-cum)
    winv = jnp.exp(cum)
    rh_s[...] = (r * wc).astype(BF16)
    ah_s[...] = (-kk * jnp.exp(dexp - cum)).astype(BF16)
    bt_s[...] = (kk * alr * winv).astype(BF16)
    kt_s[...] = (kh * winv).astype(BF16)
    v_s[...] = v.astype(BF16)
    for ci in range(nchunk):
        wl_s[ci:ci + 1, :] = wc[(ci + 1) * c - 1:(ci + 1) * c, :]

    ri = lax.broadcasted_iota(jnp.int32, (gc, gc), 0)
    cj = lax.broadcasted_iota(jnp.int32, (gc, gc), 1)
    same = (ri // c) == (cj // c)
    strict = same & ((ri % c) > (cj % c))
    incl = same & ((ri % c) >= (cj % c))
    eye = (ri == cj).astype(F32)
    smask = (lax.broadcasted_iota(jnp.int32, (gc, GL), 0) // c) == (
        lax.broadcasted_iota(jnp.int32, (gc, GL), 1) // HD)

    def stack(x):
        return jnp.where(smask, jnp.concatenate([x] * G, axis=0), jnp.zeros((), x.dtype))

    def nt(a, b):
        return lax.dot_general(a, b, _NT, preferred_element_type=F32)

    def tn(a, b):
        return lax.dot_general(a, b, _TN, preferred_element_type=F32)

    groups = range(NG)
    lanes = [slice(g * GL, (g + 1) * GL) for g in groups]

    def chunk_step(ci, carry):
        rs = pl.ds(pl.multiple_of(ci * c, c), c)
        ah = [ah_s[rs, ls] for ls in lanes]
        rh = [rh_s[rs, ls] for ls in lanes]
        bt = [bt_s[rs, ls] for ls in lanes]
        kt = [kt_s[rs, ls] for ls in lanes]
        a_st = [stack(t) for t in ah]
        b_rep = [jnp.concatenate([t] * G, axis=0) for t in bt]
        k_rep = [jnp.concatenate([t] * G, axis=0) for t in kt]
        l_ab = [jnp.where(strict, nt(a_st[g], b_rep[g]), 0.0) for g in groups]
        lp = [t.astype(BF16) for t in l_ab]
        p = [eye + t for t in l_ab]
        a_ak = [jnp.where(strict, nt(a_st[g], k_rep[g]), 0.0).astype(BF16) for g in groups]
        r_st = [stack(t) for t in rh]
        a_rb = [jnp.where(incl, nt(r_st[g], b_rep[g]), 0.0).astype(BF16) for g in groups]
        a_rk = [jnp.where(incl, nt(r_st[g], k_rep[g]), 0.0).astype(BF16) for g in groups]
        v_st = [stack(v_s[rs, ls]) for ls in lanes]
        st = [st_s[g] for g in groups]
        ars = [nt(jnp.concatenate([ah[g], rh[g]], axis=0), st[g].astype(BF16)) for g in groups]
        x_st = [stack(ars[g][:c]) + _dot(a_ak[g], v_st[g]) for g in groups]
        n = 1
        while 2 * n < c:
            lp = [_dot(t, t).astype(BF16) for t in lp]
            p = [p[g] + _dot(p[g].astype(BF16), lp[g]) for g in groups]
            n *= 2
        u_st = [_dot(p[g].astype(BF16), x_st[g].astype(BF16)).astype(BF16) for g in groups]
        y_st = [_dot(a_rb[g], u_st[g]) + _dot(a_rk[g], v_st[g]) for g in groups]
        for g in groups:
            y = ars[g][c:]
            for hh in range(G):
                y = y + y_st[g][hh * c:(hh + 1) * c]
            y_s[rs, lanes[g]] = y
        b_st = [stack(t) for t in bt]
        k_st = [stack(t) for t in kt]
        for g in groups:
            upd = tn(u_st[g], b_st[g]) + tn(v_st[g], k_st[g])
            st_s[g] = (st[g] + upd) * wl_s[pl.ds(ci, 1), lanes[g]]
        return carry

    lax.fori_loop(0, nchunk, chunk_step, 0)

    u = proj_ref[0, :, DS + D:DS + 2 * D] * proj_ref[0, :, DS + 2 * D:DS + 3 * D]
    c0 = cv_ref[0, 0:1, :]
    c1 = cv_ref[0, 1:2, :]
    u1 = jnp.where(row == 0, c1, pltpu.roll(u, 1, 0))
    u2 = jnp.where(row == 0, c0, jnp.where(row == 1, c1, pltpu.roll(u, 2, 0)))
    z = cw_ref[0:1, :] * u2 + cw_ref[1:2, :] * u1 + cw_ref[2:3, :] * u
    o_ref[0] = _mix_out(y_s[...], bonus, gate, proj_ref[0, :, DS:DS + D], z,
                        proj_ref[0, :, DS + 3 * D:DS + 4 * D], proj_ref[0, :, DS + 4 * D:DP],
                        x_ref[0], lnw_ref, lnb_ref, wo_ref, e)
    sh_ref[0] = new_shift
    cv_ref[0] = u[tl - 2:tl, :]

    @pl.when(j == pl.num_programs(1) - 1)
    def _():
        for h in range(H):
            d0 = (h % G) * HD
            so_ref[0, h] = st_s[h // G, d0:d0 + HD, d0:d0 + HD]


def _mix_seq(proj, x, s0, sh0, cv0, l, wts, e, *, tl, c):
    bsz, length, _ = x.shape
    assert length % tl == 0 and tl % c == 0 and tl >= 2
    nchunk = tl // c
    return pl.pallas_call(
        functools.partial(_mix_seq_body, tl=tl, c=c),
        grid=(bsz, length // tl),
        in_specs=[
            pl.BlockSpec((1, tl, DP), lambda b, j: (b, j, 0)),
            pl.BlockSpec((1, tl, D), lambda b, j: (b, j, 0)),
            _const_spec((1, H, HD, HD)),
            _const_spec((1, 1, DS)),
            _const_spec((1, 2, D)),
        ] + [_layer_spec(w, l) for w in wts] + [_const_spec(e.shape)],
        out_specs=[
            pl.BlockSpec((1, tl, D), lambda b, j: (b, j, 0)),
            pl.BlockSpec((1, H, HD, HD), lambda b, j: (b, 0, 0, 0)),
            pl.BlockSpec((1, 1, DS), lambda b, j: (b, 0, 0)),
            pl.BlockSpec((1, 2, D), lambda b, j: (b, 0, 0)),
        ],
        out_shape=[
            jax.ShapeDtypeStruct((bsz, length, D), F32),
            jax.ShapeDtypeStruct((bsz, H, HD, HD), F32),
            jax.ShapeDtypeStruct((bsz, 1, DS), F32),
            jax.ShapeDtypeStruct((bsz, 2, D), F32),
        ],
        scratch_shapes=[pltpu.VMEM((NG, GL, GL), F32)] + [pltpu.VMEM((tl, D), BF16)] * 5 + [
            pltpu.VMEM((max(8, nchunk), D), F32),
            pltpu.VMEM((tl, D), F32),
        ],
        compiler_params=pltpu.CompilerParams(
            dimension_semantics=("arbitrary", "arbitrary"), vmem_limit_bytes=VMEM_LIMIT),
        name=f"mix_seq_{tl}",
    )(proj, x, s0, sh0, cv0, *wts, e)


def _mix_step_body(*refs, bt, layer, aliased):
    (proj_ref, x_ref, s_ref, sh_ref, cv_ref,
     mu_ref, w0_ref, w2_ref, a0_ref, a2_ref, wg2_ref, kk_ref, ka_ref, rk_ref,
     lnw_ref, lnb_ref, cw_ref, wo_ref, e_ref) = refs[:19]
    refs = refs[20:] if aliased else refs[19:]
    o_ref, so_ref, sho_ref, cvo_ref = refs[:4]
    a_s, wr_s, w_s, b_s, k_s, v_s, s1_s, s2_s, z_s = refs[4:]

    def compute():
        e = e_ref[...]

        def cur(a, b):
            return proj_ref[:, a:b]

        def prev(a, b):
            return sh_ref[:, a:b]

        r, kh, v, kk, alr, dexp, gate, bonus = _mix_heads(
            cur, prev, mu_ref, w0_ref, w2_ref, a0_ref, a2_ref, wg2_ref, kk_ref, ka_ref, rk_ref, e)
        w = jnp.exp(-dexp)
        bb = kk * alr
        lane = lax.broadcasted_iota(jnp.int32, (bt, LANES), 1)

        def fill(dst, val):
            for p in range(H // 2):
                blk = val[:, p * LANES:(p + 1) * LANES]
                rolled = pltpu.roll(blk, HD, 1)
                dst[2 * p] = jnp.where(lane < HD, blk, rolled)
                dst[2 * p + 1] = jnp.where(lane < HD, rolled, blk)

        fill(a_s, -kk)
        fill(wr_s, w * r)
        fill(w_s, w)
        fill(b_s, bb)
        fill(k_s, kh)
        fill(v_s, v)
        fill(s1_s, _seg_sum(bb * r, e))
        fill(s2_s, _seg_sum(kh * r, e))

        li = lax.broadcasted_iota(jnp.int32, (LANES, LANES), 0)
        lj = lax.broadcasted_iota(jnp.int32, (LANES, LANES), 1)
        half_sum = ((li // HD) == (lj // HD)).astype(BF16)
        half_sum2 = jnp.concatenate([half_sum, half_sum], axis=0)
        src_v = 2 * (li % HD) + li // HD
        perm_lo = ((li % HD < PK) & (lj == src_v)).astype(BF16)
        perm_hi = ((li % HD < PK) & (lj == src_v + HD)).astype(BF16)
        pi = lax.broadcasted_iota(jnp.int32, (PK, LANES), 0)
        pj = lax.broadcasted_iota(jnp.int32, (PK, LANES), 1)
        pick_v = ((pj % HD) == 2 * pi + pj // HD).astype(F32)
        diag = ((pj % HD) == pi).astype(F32)

        def seq_step(b, carry):
            rb = pl.ds(b, 1)
            xa, xw, xv = [], [], []
            for h in range(H):
                s = s_ref[b, h]
                xa.append((s * a_s[h, rb, :]).astype(BF16))
                xw.append((s * wr_s[h, rb, :]).astype(BF16))
                vm = pick_v * v_s[h, rb, :]
                vhi = vm.astype(BF16)
                xv.append(jnp.concatenate([vhi, (vm - vhi.astype(F32)).astype(BF16)], axis=1))
            sa = _dot(jnp.concatenate(xa, axis=0), half_sum)
            swr = _dot(jnp.concatenate(xw, axis=0), half_sum)
            vb = _dot(jnp.concatenate(xv, axis=0), half_sum2)
            for h in range(H):
                hs = slice(h * PK, (h + 1) * PK)
                so_ref[b, h] = (s_ref[b, h] * w_s[h, rb, :] + sa[hs] * b_s[h, rb, :]
                                + vb[hs] * k_s[h, rb, :])
                y = swr[hs] + sa[hs] * s1_s[h, rb, :] + vb[hs] * s2_s[h, rb, :]
                z_s[h, rb, :] = jnp.sum(y * diag, axis=0, keepdims=True)
            return carry

        lax.fori_loop(0, bt, seq_step, 0)
        y = jnp.concatenate(
            [_dot_split(z_s[2 * p], perm_lo) + _dot_split(z_s[2 * p + 1], perm_hi) for p in range(H // 2)],
            axis=1)

        u = proj_ref[:, DS + D:DS + 2 * D] * proj_ref[:, DS + 2 * D:DS + 3 * D]
        c0 = cv_ref[:, 0, :]
        c1 = cv_ref[:, 1, :]
        z = cw_ref[0:1, :] * c0 + cw_ref[1:2, :] * c1 + cw_ref[2:3, :] * u
        o_ref[...] = _mix_out(y, bonus, gate, proj_ref[:, DS:DS + D], z,
                              proj_ref[:, DS + 3 * D:DS + 4 * D], proj_ref[:, DS + 4 * D:DP],
                              x_ref[...], lnw_ref, lnb_ref, wo_ref, e)
        sho_ref[...] = proj_ref[:, 0:DS]
        cvo_ref[:, 0, :] = c1
        cvo_ref[:, 1, :] = u

    if aliased:
        compute()
    else:
        slot = pl.program_id(1)
        pl.when(slot == layer)(compute)

        @pl.when(slot != layer)
        def _():
            so_ref[...] = jnp.zeros_like(so_ref)


def _mix_step(proj, x, s_all, sh_all, cv_all, stacked, l, wts, e):
    bsz = x.shape[0]
    depth = s_all.shape[0]
    bt = STEP_TILE
    assert bsz % bt == 0
    aliased = stacked is not None
    state_block = (None, bt, H, PK, LANES)
    if aliased:
        grid = (bsz // bt,)
        sem = ("arbitrary",)
        out_state = pl.BlockSpec(state_block, lambda i: (l, i, 0, 0, 0))
    else:
        grid = (bsz // bt, depth)
        sem = ("arbitrary", "arbitrary")
        out_state = pl.BlockSpec(state_block, lambda i, s: (s, i, 0, 0, 0))
    in_specs = [
        pl.BlockSpec((bt, DP), lambda i, *_: (i, 0)),
        pl.BlockSpec((bt, D), lambda i, *_: (i, 0)),
        pl.BlockSpec(state_block, lambda i, *_: (l, i, 0, 0, 0)),
        pl.BlockSpec((None, bt, DS), lambda i, *_: (l, i, 0)),
        pl.BlockSpec((None, bt, 2, D), lambda i, *_: (l, i, 0, 0)),
    ] + [_layer_spec(w, l) for w in wts] + [_const_spec(e.shape)]
    args = [proj, x, s_all, sh_all, cv_all, *wts, e]
    if aliased:
        in_specs.append(pl.BlockSpec(memory_space=pl.ANY))
        args.append(stacked)
    return pl.pallas_call(
        functools.partial(_mix_step_body, bt=bt, layer=l, aliased=aliased),
        grid=grid,
        in_specs=in_specs,
        out_specs=[
            pl.BlockSpec((bt, D), lambda i, *_: (i, 0)),
            out_state,
            pl.BlockSpec((bt, DS), lambda i, *_: (i, 0)),
            pl.BlockSpec((bt, 2, D), lambda i, *_: (i, 0, 0)),
        ],
        out_shape=[
            jax.ShapeDtypeStruct((bsz, D), F32),
            jax.ShapeDtypeStruct(s_all.shape, F32),
            jax.ShapeDtypeStruct((bsz, DS), F32),
            jax.ShapeDtypeStruct((bsz, 2, D), F32),
        ],
        scratch_shapes=[pltpu.VMEM((H, bt, LANES), F32)] * 9,
        input_output_aliases={len(args) - 1: 1} if aliased else {},
        compiler_params=pltpu.CompilerParams(dimension_semantics=sem, vmem_limit_bytes=VMEM_LIMIT),
        name="mix_step",
    )(*args)


def kernel(x_prompt, x_sample, state_wkv, state_shift, state_conv, meta_tokens, norm_ffn1, ffn1_w_in, ffn1_w_out, norm_mix, w_in, mu_shift, w0, w_w2, a0, w_a2, w_g2, k_k, k_a, r_k, lnx_w, lnx_b, conv_w, w_o, norm_ffn2, ffn2_w_in, ffn2_w_out, norm_final):
    depth = norm_ffn1.shape[0]
    bp, lp, _ = x_prompt.shape
    bs = x_sample.shape[0]
    n_meta = meta_tokens.shape[0]

    def vec(a):
        return a.reshape(depth, 1, -1)

    half_zeros = jnp.zeros((depth, LORA_WA // 2, D), F32)
    wts = tuple(dict(
        mu=vec(mu_shift), w0=vec(w0),
        w2=jnp.concatenate([w_w2, half_zeros], axis=1).astype(BF16),
        a0=vec(a0),
        a2=jnp.concatenate([half_zeros, w_a2], axis=1).astype(BF16),
        wg2=w_g2.astype(BF16), k_k=vec(k_k), k_a=vec(k_a), r_k=vec(r_k), lnw=vec(lnx_w),
        lnb=vec(lnx_b), cw=conv_w, wo=w_o.astype(BF16))[name] for name in _MIX_WEIGHTS)
    lane = jnp.arange(GL)
    e = (lane[:, None] // HD == lane[None, :] // HD).astype(BF16)
    n1, nm, n2 = vec(norm_ffn1), vec(norm_mix), vec(norm_ffn2)
    f1_in, f1_out = ffn1_w_in.astype(BF16), ffn1_w_out.astype(BF16)
    f2_in, f2_out = ffn2_w_in.astype(BF16), ffn2_w_out.astype(BF16)
    w_in_b = w_in.astype(BF16)
    gfin = norm_final[None]

    xm = meta_tokens.astype(x_prompt.dtype)
    xp = x_prompt.reshape(bp * lp, D)
    xs = x_sample.reshape(bs, D)
    s_all = state_wkv.reshape(depth, bs, H, PK, LANES)
    zero_s = jnp.zeros((1, H, HD, HD), F32)
    zero_sh = jnp.zeros((1, 1, DS), F32)
    zero_cv = jnp.zeros((1, 2, D), F32)
    p_wkv, p_shift, p_conv, s_shift, s_conv = [], [], [], [], []
    s_new = None
    for l in range(depth):
        xm = _ffn(xm, l, n1, f1_in, f1_out, gfin, False)
        xp = _ffn(xp, l, n1, f1_in, f1_out, gfin, False)
        xs = _ffn(xs, l, n1, f1_in, f1_out, gfin, False)
        pm = _proj(xm, l, nm, w_in_b)
        pp = _proj(xp, l, nm, w_in_b)
        psm = _proj(xs, l, nm, w_in_b)
        xm3, m_s, m_sh, m_cv = _mix_seq(pm.reshape(1, n_meta, DP), xm.reshape(1, n_meta, D), zero_s,
                                        zero_sh, zero_cv, l, wts, e, tl=n_meta, c=n_meta)
        xp3, p_s, p_sh, p_cv = _mix_seq(pp.reshape(bp, lp, DP), xp.reshape(bp, lp, D), m_s, m_sh, m_cv,
                                        l, wts, e, tl=min(SEQ_TILE, lp), c=CHUNK)
        xs, s_new, s_sh, s_cv = _mix_step(psm, xs, s_all, state_shift, state_conv, s_new, l, wts, e)
        xm = xm3.reshape(n_meta, D)
        xp = xp3.reshape(bp * lp, D)
        last = l == depth - 1
        if not last:
            xm = _ffn(xm, l, n2, f2_in, f2_out, gfin, False)
        xp = _ffn(xp, l, n2, f2_in, f2_out, gfin, last)
        xs = _ffn(xs, l, n2, f2_in, f2_out, gfin, last)
        p_wkv.append(p_s)
        p_shift.append(p_sh[:, 0])
        p_conv.append(p_cv)
        s_shift.append(s_sh)
        s_conv.append(s_cv)
    return (xp.reshape(bp, lp, D), xs.reshape(bs, 1, D), jnp.stack(p_wkv), jnp.stack(p_shift),
            jnp.stack(p_conv), s_new.reshape(state_wkv.shape), jnp.stack(s_shift), jnp.stack(s_conv))
```

```python
import functools
import math

import jax
import jax.numpy as jnp
from jax import lax
from jax.experimental import pallas as pl
from jax.experimental.pallas import tpu as pltpu

F32 = jnp.float32
BF16 = jnp.bfloat16

D = 1024
H = 16
HD = 64
G = 4
GL = G * HD
NG = H // G
LORA_WA = 128
LORA_G = 128
DS = 3 * D + LORA_WA + LORA_G
DP = DS + 3 * D + 2 * D
DFF = 2816
RMS_EPS = 1e-6
GN_EPS = 64e-5
EXP_M05 = math.exp(-0.5)
LANES = 128
MXU_DIM = 256
PK = HD * HD // LANES

CHUNK = 32
SEQ_TILE = 256
ROW_TILE = 512
PROJ_TILE = 256
STEP_TILE = 16
FF_SPLIT = 2
VMEM_LIMIT = 56 * 1024 * 1024

_NT = (((1,), (1,)), ((), ()))
_TN = (((0,), (0,)), ((), ()))

_MIX_WEIGHTS = ("w0", "w2", "a0", "a2", "wg2", "k_k", "k_a", "r_k", "lnw", "lnb", "wo")


def _dot(a, b):
    return jnp.dot(a, b, preferred_element_type=F32)


def _dot_split(x, m):
    hi = x.astype(BF16)
    lo = (x - hi.astype(F32)).astype(BF16)
    return _dot(hi, m) + _dot(lo, m)


def _rmsnorm(x, g):
    return x * lax.rsqrt(jnp.mean(x * x, axis=-1, keepdims=True) + RMS_EPS) * g


def _sigmoid(x):
    return 1.0 / (1.0 + jnp.exp(-x))


def _const_spec(shape):
    nd = len(shape)
    return pl.BlockSpec(shape, lambda *_: (0,) * nd)


def _layer_spec(arr, l, **kw):
    nd = arr.ndim - 1
    return pl.BlockSpec((None,) + arr.shape[1:], lambda *_: (l,) + (0,) * nd, **kw)


def _ffn_body(x_ref, g_ref, win_ref, wout_ref, gf_ref, o_ref, *, final_norm):
    x = x_ref[...]
    xb = _rmsnorm(x, g_ref[...]).astype(BF16)
    piece = DFF // FF_SPLIT
    acc = None
    for c in range(FF_SPLIT):
        gate = _dot(xb, win_ref[:, c * piece:(c + 1) * piece])
        up = _dot(xb, win_ref[:, DFF + c * piece:DFF + (c + 1) * piece])
        act = (gate * _sigmoid(gate) * up).astype(BF16)
        part = _dot(act, wout_ref[c * piece:(c + 1) * piece, :])
        acc = part if acc is None else acc + part
    y = x + 0.5 * acc
    if final_norm:
        y = _rmsnorm(y, gf_ref[...])
    o_ref[...] = y


def _ffn(x, l, g, w_in, w_out, g_final, final_norm):
    rows = x.shape[0]
    tm = min(ROW_TILE, rows)
    assert rows % tm == 0
    return pl.pallas_call(
        functools.partial(_ffn_body, final_norm=final_norm),
        grid=(rows // tm,),
        in_specs=[
            pl.BlockSpec((tm, D), lambda i: (i, 0)),
            _layer_spec(g, l),
            _layer_spec(w_in, l, pipeline_mode=pl.Buffered(1)),
            _layer_spec(w_out, l, pipeline_mode=pl.Buffered(1)),
            _const_spec((1, D)),
        ],
        out_specs=pl.BlockSpec((tm, D), lambda i: (i, 0)),
        out_shape=jax.ShapeDtypeStruct((rows, D), F32),
        compiler_params=pltpu.CompilerParams(
            dimension_semantics=("arbitrary",), vmem_limit_bytes=VMEM_LIMIT),
        name="ffn",
    )(x, g, w_in, w_out, g_final)


def _proj_columns(x_ref, g_ref, w_ref):
    xb = _rmsnorm(x_ref[...], g_ref[...]).astype(BF16)
    ps = _dot(xb, w_ref[:, 0:DS])
    pc = _dot(xb, w_ref[:, DS:DS + 3 * D])
    pg = _dot(xb, w_ref[:, DS + 3 * D:DP])
    return ps, pc, pg


def _proj_seq_body(x_ref, g_ref, w_ref, mu_ref, cw_ref, sh0_ref, cv0_ref,
                   xs_ref, mb_ref, sa_ref, sh_ref, cv_ref, shc_s, cvc_s, *, tm, tps):
    i = pl.program_id(0)

    @pl.when(i % tps == 0)
    def _():
        shc_s[...] = sh0_ref[0]
        cvc_s[...] = cv0_ref[0]

    ps, pc, pg = _proj_columns(x_ref, g_ref, w_ref)
    row = lax.broadcasted_iota(jnp.int32, (tm, 1), 0)
    prev = jnp.where(row == 0, shc_s[...], pltpu.roll(ps, 1, 0))
    xs_ref[...] = ps + (prev - ps) * mu_ref[...]
    last_ps = ps[tm - 1:tm, :]
    shc_s[...] = last_ps

    u = pc[:, D:2 * D] * pc[:, 2 * D:3 * D]
    c0 = cvc_s[0:1, :]
    c1 = cvc_s[1:2, :]
    u1 = jnp.where(row == 0, c1, pltpu.roll(u, 1, 0))
    u2 = jnp.where(row == 0, c0, jnp.where(row == 1, c1, pltpu.roll(u, 2, 0)))
    z = cw_ref[0:1, :] * u2 + cw_ref[1:2, :] * u1 + cw_ref[2:3, :] * u
    mb_ref[...] = _sigmoid(pg[:, D:2 * D]) * (pc[:, 0:D] * z)
    sa_ref[...] = _sigmoid(pg[:, 0:D])
    last_u = u[tm - 2:tm, :]
    cvc_s[...] = last_u

    @pl.when(i % tps == tps - 1)
    def _():
        sh_ref[0] = last_ps
        cv_ref[0] = last_u


def _proj_seq(x, seq_len, l, g, w, mu, cw, sh0, cv0):
    rows = x.shape[0]
    tm = min(PROJ_TILE, seq_len)
    assert rows % seq_len == 0 and seq_len % tm == 0 and tm >= 2
    nseq, tps = rows // seq_len, seq_len // tm
    return pl.pallas_call(
        functools.partial(_proj_seq_body, tm=tm, tps=tps),
        grid=(rows // tm,),
        in_specs=[
            pl.BlockSpec((tm, D), lambda i: (i, 0)),
            _layer_spec(g, l),
            _layer_spec(w, l, pipeline_mode=pl.Buffered(1)),
            _layer_spec(mu, l),
            _layer_spec(cw, l),
            _const_spec((1, 1, DS)),
            _const_spec((1, 2, D)),
        ],
        out_specs=[
            pl.BlockSpec((tm, DS), lambda i: (i, 0)),
            pl.BlockSpec((tm, D), lambda i: (i, 0)),
            pl.BlockSpec((tm, D), lambda i: (i, 0)),
            pl.BlockSpec((1, 1, DS), lambda i: (i // tps, 0, 0)),
            pl.BlockSpec((1, 2, D), lambda i: (i // tps, 0, 0)),
        ],
        out_shape=[
            jax.ShapeDtypeStruct((rows, DS), F32),
            jax.ShapeDtypeStruct((rows, D), F32),
            jax.ShapeDtypeStruct((rows, D), F32),
            jax.ShapeDtypeStruct((nseq, 1, DS), F32),
            jax.ShapeDtypeStruct((nseq, 2, D), F32),
        ],
        scratch_shapes=[pltpu.VMEM((1, DS), F32), pltpu.VMEM((2, D), F32)],
        compiler_params=pltpu.CompilerParams(
            dimension_semantics=("arbitrary",), vmem_limit_bytes=VMEM_LIMIT),
        name="proj_seq",
    )(x, g, w, mu, cw, sh0, cv0)


def _proj_step_body(x_ref, g_ref, w_ref, mu_ref, cw_ref, sh_ref, cv_ref,
                    xs_ref, mb_ref, sa_ref, sho_ref, cvo_ref):
    ps, pc, pg = _proj_columns(x_ref, g_ref, w_ref)
    xs_ref[...] = ps + (sh_ref[...] - ps) * mu_ref[...]
    sho_ref[...] = ps
    u = pc[:, D:2 * D] * pc[:, 2 * D:3 * D]
    c1 = cv_ref[:, 1, :]
    z = cw_ref[0:1, :] * cv_ref[:, 0, :] + cw_ref[1:2, :] * c1 + cw_ref[2:3, :] * u
    mb_ref[...] = _sigmoid(pg[:, D:2 * D]) * (pc[:, 0:D] * z)
    sa_ref[...] = _sigmoid(pg[:, 0:D])
    cvo_ref[:, 0, :] = c1
    cvo_ref[:, 1, :] = u


def _proj_step(x, l, g, w, mu, cw, sh_all, cv_all):
    rows = x.shape[0]
    return pl.pallas_call(
        _proj_step_body,
        grid=(1,),
        in_specs=[
            _const_spec((rows, D)),
            _layer_spec(g, l),
            _layer_spec(w, l, pipeline_mode=pl.Buffered(1)),
            _layer_spec(mu, l),
            _layer_spec(cw, l),
            _layer_spec(sh_all, l),
            _layer_spec(cv_all, l),
        ],
        out_specs=[
            _const_spec((rows, DS)),
            _const_spec((rows, D)),
            _const_spec((rows, D)),
            _const_spec((rows, DS)),
            _const_spec((rows, 2, D)),
        ],
        out_shape=[
            jax.ShapeDtypeStruct((rows, DS), F32),
            jax.ShapeDtypeStruct((rows, D), F32),
            jax.ShapeDtypeStruct((rows, D), F32),
            jax.ShapeDtypeStruct((rows, DS), F32),
            jax.ShapeDtypeStruct((rows, 2, D), F32),
        ],
        compiler_params=pltpu.CompilerParams(
            dimension_semantics=("arbitrary",), vmem_limit_bytes=VMEM_LIMIT),
        name="proj_step",
    )(x, g, w, mu, cw, sh_all, cv_all)


def _seg_sum(x, e):
    return jnp.concatenate([_dot_split(x[:, g * GL:(g + 1) * GL], e) for g in range(NG)], axis=1)


def _mix_heads(xs, w0_ref, w2_ref, a0_ref, a2_ref, wg2_ref, kk_ref, ka_ref, rk_ref, e):
    r = xs(0, D)
    k = xs(D, 2 * D)
    v = xs(2 * D, 3 * D)
    lwla = xs(3 * D, 3 * D + LORA_WA)
    lg = xs(3 * D + LORA_WA, DS)
    wl = w0_ref[...] + _dot(jnp.tanh(lwla).astype(BF16), w2_ref[...])
    dexp = EXP_M05 * _sigmoid(wl)
    alr = _sigmoid(a0_ref[...] + _dot(lwla.astype(BF16), a2_ref[...]))
    gate = _dot(_sigmoid(lg).astype(BF16), wg2_ref[...])
    kkr = k * kk_ref[...]
    kk = kkr * lax.rsqrt(jnp.maximum(_seg_sum(kkr * kkr, e), 1e-24))
    kh = k * (1.0 + (alr - 1.0) * ka_ref[...])
    bonus = _seg_sum(r * kh * rk_ref[...], e) * v
    return r, kh, v, kk, alr, dexp, gate, bonus


def _mix_out(y, bonus, gate, sa, mb, x, lnw_ref, lnb_ref, wo_ref, e):
    mu = _seg_sum(y, e) * (1.0 / HD)
    d = y - mu
    var = _seg_sum(d * d, e) * (1.0 / HD)
    yn = d * lax.rsqrt(var + GN_EPS) * lnw_ref[...] + lnb_ref[...]
    m = sa * ((yn + bonus) * gate) + mb
    return x + _dot(m.astype(BF16), wo_ref[...])


def _mix_seq_body(xs_ref, mb_ref, sa_ref, x_ref, s0_ref,
                  w0_ref, w2_ref, a0_ref, a2_ref, wg2_ref, kk_ref, ka_ref, rk_ref,
                  lnw_ref, lnb_ref, wo_ref, e_ref,
                  o_ref, so_ref,
                  st_s, ah_s, rh_s, bt_s, kt_s, v_s, wl_s, y_s, *, tl, c):
    j = pl.program_id(1)
    gc = G * c
    nchunk = tl // c

    @pl.when(j == 0)
    def _():
        st_s[...] = jnp.zeros_like(st_s)
        for h in range(H):
            d0 = (h % G) * HD
            st_s[h // G, d0:d0 + HD, d0:d0 + HD] = s0_ref[0, h]

    e = e_ref[...]
    row = lax.broadcasted_iota(jnp.int32, (tl, 1), 0)
    r, kh, v, kk, alr, dexp, gate, bonus = _mix_heads(
        lambda a, b: xs_ref[0, :, a:b],
        w0_ref, w2_ref, a0_ref, a2_ref, wg2_ref, kk_ref, ka_ref, rk_ref, e)

    rin = row % c
    cum = dexp
    s = 1
    while s < c:
        cum = cum + jnp.where(rin >= s, pltpu.roll(cum, s, 0), 0.0)
        s *= 2
    wc = jnp.exp(-cum)
    winv = jnp.exp(cum)
    rh_s[...] = (r * wc).astype(BF16)
    ah_s[...] = (-kk * jnp.exp(dexp - cum)).astype(BF16)
    bt_s[...] = (kk * alr * winv).astype(BF16)
    kt_s[...] = (kh * winv).astype(BF16)
    v_s[...] = v.astype(BF16)
    for ci in range(nchunk):
        wl_s[ci:ci + 1, :] = wc[(ci + 1) * c - 1:(ci + 1) * c, :]

    ri = lax.broadcasted_iota(jnp.int32, (gc, gc), 0)
    cj = lax.broadcasted_iota(jnp.int32, (gc, gc), 1)
    same = (ri // c) == (cj // c)
    strict = same & ((ri % c) > (cj % c))
    incl = same & ((ri % c) >= (cj % c))
    eye = (ri == cj).astype(F32)
    smask = (lax.broadcasted_iota(jnp.int32, (gc, GL), 0) // c) == (
        lax.broadcasted_iota(jnp.int32, (gc, GL), 1) // HD)

    def stack(x):
        return jnp.where(smask, jnp.concatenate([x] * G, axis=0), jnp.zeros((), x.dtype))

    def nt(a, b):
        return lax.dot_general(a, b, _NT, preferred_element_type=F32)

    def tn(a, b):
        return lax.dot_general(a, b, _TN, preferred_element_type=F32)

    groups = range(NG)
    lanes = [slice(g * GL, (g + 1) * GL) for g in groups]

    merged = gc % LANES == 0 and 2 * gc <= MXU_DIM

    def state_free_stages(q):
        rs = slice(q * c, (q + 1) * c)
        t = {"rs": rs}

        def s_products_a():
            t["ah"] = [ah_s[rs, ls] for ls in lanes]
            t["bt"] = [bt_s[rs, ls] for ls in lanes]
            t["kt"] = [kt_s[rs, ls] for ls in lanes]
            a_st = [stack(x) for x in t["ah"]]
            b_rep = [jnp.concatenate([x] * G, axis=0) for x in t["bt"]]
            k_rep = [jnp.concatenate([x] * G, axis=0) for x in t["kt"]]
            if merged:
                t["bk"] = [jnp.concatenate([b_rep[g], k_rep[g]], axis=0) for g in groups]
                ga = [nt(a_st[g], t["bk"][g]) for g in groups]
                l_ab = [jnp.where(strict, ga[g][:, :gc], 0.0) for g in groups]
                t["a_ak"] = [jnp.where(strict, ga[g][:, gc:], 0.0).astype(BF16) for g in groups]
            else:
                t["b_rep"], t["k_rep"] = b_rep, k_rep
                l_ab = [jnp.where(strict, nt(a_st[g], b_rep[g]), 0.0) for g in groups]
                t["a_ak"] = [jnp.where(strict, nt(a_st[g], k_rep[g]), 0.0).astype(BF16) for g in groups]
            t["lp"] = [x.astype(BF16) for x in l_ab]
            t["p"] = [eye + x for x in l_ab]

        def s_products_r():
            t["rh"] = [rh_s[rs, ls] for ls in lanes]
            r_st = [stack(x) for x in t["rh"]]
            if merged:
                gr = [nt(r_st[g], t["bk"][g]) for g in groups]
                t["a_r"] = [jnp.concatenate([jnp.where(incl, gr[g][:, :gc], 0.0),
                                             jnp.where(incl, gr[g][:, gc:], 0.0)], axis=1).astype(BF16)
                            for g in groups]
            else:
                t["a_rb"] = [jnp.where(incl, nt(r_st[g], t["b_rep"][g]), 0.0).astype(BF16) for g in groups]
                t["a_rk"] = [jnp.where(incl, nt(r_st[g], t["k_rep"][g]), 0.0).astype(BF16) for g in groups]

        def s_av():
            t["v_st"] = [stack(v_s[rs, ls]) for ls in lanes]
            t["av"] = [_dot(t["a_ak"][g], t["v_st"][g]) for g in groups]

        def s_square():
            t["lp"] = [_dot(x, x).astype(BF16) for x in t["lp"]]

        def s_extend():
            t["p"] = [t["p"][g] + _dot(t["p"][g].astype(BF16), t["lp"][g]) for g in groups]

        def s_finish():
            t["pb"] = [x.astype(BF16) for x in t["p"]]
            t["b_st"] = [stack(x) for x in t["bt"]]
            t["k_st"] = [stack(x) for x in t["kt"]]

        stages = [s_products_a, s_products_r, s_av]
        n = 1
        while 2 * n < c:
            stages += [s_square, s_extend]
            n *= 2
        return t, stages + [s_finish]

    def state_chain_stages(q, t, st):
        rs = t["rs"]
        w = {}

        def s_read_state():
            w["ars"] = [nt(jnp.concatenate([t["ah"][g], t["rh"][g]], axis=0), st[g].astype(BF16))
                        for g in groups]

        def s_solve():
            x_st = [stack(w["ars"][g][:c]) + t["av"][g] for g in groups]
            w["u"] = [_dot(t["pb"][g], x_st[g].astype(BF16)).astype(BF16) for g in groups]

        def s_outputs():
            if merged:
                uv = [jnp.concatenate([w["u"][g], t["v_st"][g]], axis=0) for g in groups]
                w["y"] = [_dot(t["a_r"][g], uv[g]) for g in groups]
                w["upd"] = [tn(uv[g], jnp.concatenate([t["b_st"][g], t["k_st"][g]], axis=0))
                            for g in groups]
            else:
                w["y"] = [_dot(t["a_rb"][g], w["u"][g]) + _dot(t["a_rk"][g], t["v_st"][g]) for g in groups]
                w["upd"] = [tn(w["u"][g], t["b_st"][g]) + tn(t["v_st"][g], t["k_st"][g]) for g in groups]

        def s_commit():
            for g in groups:
                y = w["ars"][g][c:]
                for hh in range(G):
                    y = y + w["y"][g][hh * c:(hh + 1) * c]
                y_s[rs, lanes[g]] = y
                st[g] = (st[g] + w["upd"][g]) * wl_s[q:q + 1, lanes[g]]

        return [s_read_state, s_solve, s_outputs, s_commit]

    st = [st_s[g] for g in groups]
    cur_t, first = state_free_stages(0)
    for stage in first:
        stage()
    for q in range(nchunk):
        chain = state_chain_stages(q, cur_t, st)
        nxt_t, free = state_free_stages(q + 1) if q + 1 < nchunk else (None, [])
        every = max(1, len(free) // len(chain))
        fi = 0
        for stage in chain:
            for _ in range(every):
                if fi < len(free):
                    free[fi]()
                    fi += 1
            stage()
        while fi < len(free):
            free[fi]()
            fi += 1
        cur_t = nxt_t
    for g in groups:
        st_s[g] = st[g]

    o_ref[0] = _mix_out(y_s[...], bonus, gate, sa_ref[0], mb_ref[0], x_ref[0],
                        lnw_ref, lnb_ref, wo_ref, e)

    @pl.when(j == pl.num_programs(1) - 1)
    def _():
        for h in range(H):
            d0 = (h % G) * HD
            so_ref[0, h] = st_s[h // G, d0:d0 + HD, d0:d0 + HD]


def _mix_seq(xs, mb, sa, x, s0, l, wts, e, *, tl, c):
    bsz, length, _ = x.shape
    assert length % tl == 0 and tl % c == 0
    nchunk = tl // c

    def tile(width):
        return pl.BlockSpec((1, tl, width), lambda b, j: (b, j, 0))

    return pl.pallas_call(
        functools.partial(_mix_seq_body, tl=tl, c=c),
        grid=(bsz, length // tl),
        in_specs=[tile(DS), tile(D), tile(D), tile(D), _const_spec((1, H, HD, HD))]
        + [_layer_spec(w, l) for w in wts] + [_const_spec(e.shape)],
        out_specs=[tile(D), pl.BlockSpec((1, H, HD, HD), lambda b, j: (b, 0, 0, 0))],
        out_shape=[
            jax.ShapeDtypeStruct((bsz, length, D), F32),
            jax.ShapeDtypeStruct((bsz, H, HD, HD), F32),
        ],
        scratch_shapes=[pltpu.VMEM((NG, GL, GL), F32)] + [pltpu.VMEM((tl, D), BF16)] * 5 + [
            pltpu.VMEM((max(8, nchunk), D), F32),
            pltpu.VMEM((tl, D), F32),
        ],
        compiler_params=pltpu.CompilerParams(
            dimension_semantics=("arbitrary", "arbitrary"), vmem_limit_bytes=VMEM_LIMIT),
        name=f"mix_seq_{tl}",
    )(xs, mb, sa, x, s0, *wts, e)


def _mix_step_body(*refs, bt, layer, aliased):
    (xs_ref, mb_ref, sa_ref, x_ref, s_ref,
     w0_ref, w2_ref, a0_ref, a2_ref, wg2_ref, kk_ref, ka_ref, rk_ref,
     lnw_ref, lnb_ref, wo_ref, e_ref) = refs[:17]
    refs = refs[18:] if aliased else refs[17:]
    o_ref, so_ref = refs[:2]
    a_s, wr_s, w_s, b_s, k_s, v_s, s1_s, s2_s, z_s = refs[2:]

    def compute():
        e = e_ref[...]
        r, kh, v, kk, alr, dexp, gate, bonus = _mix_heads(
            lambda a, b: xs_ref[:, a:b],
            w0_ref, w2_ref, a0_ref, a2_ref, wg2_ref, kk_ref, ka_ref, rk_ref, e)
        w = jnp.exp(-dexp)
        bb = kk * alr
        lane = lax.broadcasted_iota(jnp.int32, (bt, LANES), 1)

        def fill(dst, val):
            for p in range(H // 2):
                blk = val[:, p * LANES:(p + 1) * LANES]
                rolled = pltpu.roll(blk, HD, 1)
                dst[2 * p] = jnp.where(lane < HD, blk, rolled)
                dst[2 * p + 1] = jnp.where(lane < HD, rolled, blk)

        fill(a_s, -kk)
        fill(wr_s, w * r)
        fill(w_s, w)
        fill(b_s, bb)
        fill(k_s, kh)
        fill(v_s, v)
        fill(s1_s, _seg_sum(bb * r, e))
        fill(s2_s, _seg_sum(kh * r, e))

        li = lax.broadcasted_iota(jnp.int32, (LANES, LANES), 0)
        lj = lax.broadcasted_iota(jnp.int32, (LANES, LANES), 1)
        half_sum = ((li // HD) == (lj // HD)).astype(BF16)
        half_sum2 = jnp.concatenate([half_sum, half_sum], axis=0)
        src_v = 2 * (li % HD) + li // HD
        perm_lo = ((li % HD < PK) & (lj == src_v)).astype(BF16)
        perm_hi = ((li % HD < PK) & (lj == src_v + HD)).astype(BF16)
        pi = lax.broadcasted_iota(jnp.int32, (PK, LANES), 0)
        pj = lax.broadcasted_iota(jnp.int32, (PK, LANES), 1)
        pick_v = ((pj % HD) == 2 * pi + pj // HD).astype(F32)
        diag = ((pj % HD) == pi).astype(F32)

        def seq_step(b, carry):
            rb = pl.ds(b, 1)
            xa, xw, xv = [], [], []
            for h in range(H):
                s = s_ref[b, h]
                xa.append((s * a_s[h, rb, :]).astype(BF16))
                xw.append((s * wr_s[h, rb, :]).astype(BF16))
                vm = pick_v * v_s[h, rb, :]
                vhi = vm.astype(BF16)
                xv.append(jnp.concatenate([vhi, (vm - vhi.astype(F32)).astype(BF16)], axis=1))
            sa = _dot(jnp.concatenate(xa, axis=0), half_sum)
            swr = _dot(jnp.concatenate(xw, axis=0), half_sum)
            vb = _dot(jnp.concatenate(xv, axis=0), half_sum2)
            for h in range(H):
                hs = slice(h * PK, (h + 1) * PK)
                so_ref[b, h] = (s_ref[b, h] * w_s[h, rb, :] + sa[hs] * b_s[h, rb, :]
                                + vb[hs] * k_s[h, rb, :])
                y = swr[hs] + sa[hs] * s1_s[h, rb, :] + vb[hs] * s2_s[h, rb, :]
                z_s[h, rb, :] = jnp.sum(y * diag, axis=0, keepdims=True)
            return carry

        lax.fori_loop(0, bt, seq_step, 0)
        y = jnp.concatenate(
            [_dot_split(z_s[2 * p], perm_lo) + _dot_split(z_s[2 * p + 1], perm_hi) for p in range(H // 2)],
            axis=1)
        o_ref[...] = _mix_out(y, bonus, gate, sa_ref[...], mb_ref[...], x_ref[...],
                              lnw_ref, lnb_ref, wo_ref, e)

    if aliased:
        compute()
    else:
        slot = pl.program_id(1)
        pl.when(slot == layer)(compute)

        @pl.when(slot != layer)
        def _():
            so_ref[...] = jnp.zeros_like(so_ref)


def _mix_step(xs, mb, sa, x, s_all, stacked, l, wts, e):
    bsz = x.shape[0]
    depth = s_all.shape[0]
    bt = STEP_TILE
    assert bsz % bt == 0
    aliased = stacked is not None
    state_block = (None, bt, H, PK, LANES)
    if aliased:
        grid = (bsz // bt,)
        sem = ("arbitrary",)
        out_state = pl.BlockSpec(state_block, lambda i: (l, i, 0, 0, 0))
    else:
        grid = (bsz // bt, depth)
        sem = ("arbitrary", "arbitrary")
        out_state = pl.BlockSpec(state_block, lambda i, s: (s, i, 0, 0, 0))

    def rows(width):
        return pl.BlockSpec((bt, width), lambda i, *_: (i, 0))

    in_specs = [rows(DS), rows(D), rows(D), rows(D),
                pl.BlockSpec(state_block, lambda i, *_: (l, i, 0, 0, 0))]
    in_specs += [_layer_spec(w, l) for w in wts] + [_const_spec(e.shape)]
    args = [xs, mb, sa, x, s_all, *wts, e]
    if aliased:
        in_specs.append(pl.BlockSpec(memory_space=pl.ANY))
        args.append(stacked)
    return pl.pallas_call(
        functools.partial(_mix_step_body, bt=bt, layer=l, aliased=aliased),
        grid=grid,
        in_specs=in_specs,
        out_specs=[rows(D), out_state],
        out_shape=[
            jax.ShapeDtypeStruct((bsz, D), F32),
            jax.ShapeDtypeStruct(s_all.shape, F32),
        ],
        scratch_shapes=[pltpu.VMEM((H, bt, LANES), F32)] * 9,
        input_output_aliases={len(args) - 1: 1} if aliased else {},
        compiler_params=pltpu.CompilerParams(dimension_semantics=sem, vmem_limit_bytes=VMEM_LIMIT),
        name="mix_step",
    )(*args)


def kernel(x_prompt, x_sample, state_wkv, state_shift, state_conv, meta_tokens, norm_ffn1, ffn1_w_in, ffn1_w_out, norm_mix, w_in, mu_shift, w0, w_w2, a0, w_a2, w_g2, k_k, k_a, r_k, lnx_w, lnx_b, conv_w, w_o, norm_ffn2, ffn2_w_in, ffn2_w_out, norm_final):
    depth = norm_ffn1.shape[0]
    bp, lp, _ = x_prompt.shape
    bs = x_sample.shape[0]
    n_meta = meta_tokens.shape[0]

    def vec(a):
        return a.reshape(depth, 1, -1)

    half_zeros = jnp.zeros((depth, LORA_WA // 2, D), F32)
    wts = tuple(dict(
        w0=vec(w0),
        w2=jnp.concatenate([w_w2, half_zeros], axis=1).astype(BF16),
        a0=vec(a0),
        a2=jnp.concatenate([half_zeros, w_a2], axis=1).astype(BF16),
        wg2=w_g2.astype(BF16), k_k=vec(k_k), k_a=vec(k_a), r_k=vec(r_k), lnw=vec(lnx_w),
        lnb=vec(lnx_b), wo=w_o.astype(BF16))[name] for name in _MIX_WEIGHTS)
    lane = jnp.arange(GL)
    e = (lane[:, None] // HD == lane[None, :] // HD).astype(BF16)
    n1, nm, n2, mu = vec(norm_ffn1), vec(norm_mix), vec(norm_ffn2), vec(mu_shift)
    f1_in, f1_out = ffn1_w_in.astype(BF16), ffn1_w_out.astype(BF16)
    f2_in, f2_out = ffn2_w_in.astype(BF16), ffn2_w_out.astype(BF16)
    w_in_b = w_in.astype(BF16)
    gfin = norm_final[None]

    xm = meta_tokens.astype(x_prompt.dtype)
    xp = x_prompt.reshape(bp * lp, D)
    xs = x_sample.reshape(bs, D)
    s_all = state_wkv.reshape(depth, bs, H, PK, LANES)
    zero_s = jnp.zeros((1, H, HD, HD), F32)
    zero_sh = jnp.zeros((1, 1, DS), F32)
    zero_cv = jnp.zeros((1, 2, D), F32)
    p_wkv, p_shift, p_conv, s_shift, s_conv = [], [], [], [], []
    s_new = None
    for l in range(depth):
        xm = _ffn(xm, l, n1, f1_in, f1_out, gfin, False)
        xp = _ffn(xp, l, n1, f1_in, f1_out, gfin, False)
        xs = _ffn(xs, l, n1, f1_in, f1_out, gfin, False)
        m_xs, m_mb, m_sa, m_sh, m_cv = _proj_seq(xm, n_meta, l, nm, w_in_b, mu, conv_w, zero_sh, zero_cv)
        p_xs, p_mb, p_sa, p_sh, p_cv = _proj_seq(xp, lp, l, nm, w_in_b, mu, conv_w, m_sh, m_cv)
        s_xs, s_mb, s_sa, s_sh, s_cv = _proj_step(xs, l, nm, w_in_b, mu, conv_w, state_shift, state_conv)
        xm3, m_s = _mix_seq(m_xs.reshape(1, n_meta, DS), m_mb.reshape(1, n_meta, D),
                            m_sa.reshape(1, n_meta, D), xm.reshape(1, n_meta, D), zero_s,
                            l, wts, e, tl=n_meta, c=n_meta)
        xp3, p_s = _mix_seq(p_xs.reshape(bp, lp, DS), p_mb.reshape(bp, lp, D), p_sa.reshape(bp, lp, D),
                            xp.reshape(bp, lp, D), m_s, l, wts, e, tl=min(SEQ_TILE, lp), c=CHUNK)
        xs, s_new = _mix_step(s_xs, s_mb, s_sa, xs, s_all, s_new, l, wts, e)
        xm = xm3.reshape(n_meta, D)
        xp = xp3.reshape(bp * lp, D)
        last = l == depth - 1
        if not last:
            xm = _ffn(xm, l, n2, f2_in, f2_out, gfin, False)
        xp = _ffn(xp, l, n2, f2_in, f2_out, gfin, last)
        xs = _ffn(xs, l, n2, f2_in, f2_out, gfin, last)
        p_wkv.append(p_s)
        p_shift.append(p_sh[:, 0])
        p_conv.append(p_cv)
        s_shift.append(s_sh)
        s_conv.append(s_cv)
    return (xp.reshape(bp, lp, D), xs.reshape(bs, 1, D), jnp.stack(p_wkv), jnp.stack(p_shift),
            jnp.stack(p_conv), s_new.reshape(state_wkv.shape), jnp.stack(s_shift), jnp.stack(s_conv))
```

```python
import functools
import math

import jax
import jax.numpy as jnp
from jax import lax
from jax.experimental import pallas as pl
from jax.experimental.pallas import tpu as pltpu

F32 = jnp.float32
BF16 = jnp.bfloat16

D = 1024
H = 16
HD = 64
G = 4
GL = G * HD
NG = H // G
LORA_WA = 128
LORA_G = 128
DS = 3 * D + LORA_WA + LORA_G
DP = DS + 3 * D + 2 * D
DFF = 2816
RMS_EPS = 1e-6
GN_EPS = 64e-5
EXP_M05 = math.exp(-0.5)
LANES = 128
MXU_DIM = 256
PK = HD * HD // LANES

CHUNK = 32
SEQ_TILE = 256
ROW_TILE = 512
PROJ_TILE = 256
STEP_TILE = 16
FF_SPLIT = 2
VMEM_LIMIT = 56 * 1024 * 1024

_NT = (((1,), (1,)), ((), ()))
_TN = (((0,), (0,)), ((), ()))

_MIX_WEIGHTS = ("w0", "w2", "a0", "a2", "wg2", "k_k", "k_a", "r_k", "lnw", "lnb", "wo")


def _dot(a, b):
    return jnp.dot(a, b, preferred_element_type=F32)


def _dot_split(x, m):
    hi = x.astype(BF16)
    lo = (x - hi.astype(F32)).astype(BF16)
    return _dot(hi, m) + _dot(lo, m)


def _rmsnorm(x, g):
    return x * lax.rsqrt(jnp.mean(x * x, axis=-1, keepdims=True) + RMS_EPS) * g


def _sigmoid(x):
    return 1.0 / (1.0 + jnp.exp(-x))


def _const_spec(shape):
    nd = len(shape)
    return pl.BlockSpec(shape, lambda *_: (0,) * nd)


def _layer_spec(arr, l, **kw):
    nd = arr.ndim - 1
    return pl.BlockSpec((None,) + arr.shape[1:], lambda *_: (l,) + (0,) * nd, **kw)


def _ffn_body(x_ref, g_ref, win_ref, wout_ref, gf_ref, o_ref, *, final_norm):
    x = x_ref[...]
    xb = _rmsnorm(x, g_ref[...]).astype(BF16)
    piece = DFF // FF_SPLIT
    acc = None
    for c in range(FF_SPLIT):
        gate = _dot(xb, win_ref[:, c * piece:(c + 1) * piece])
        up = _dot(xb, win_ref[:, DFF + c * piece:DFF + (c + 1) * piece])
        act = (gate * _sigmoid(gate) * up).astype(BF16)
        part = _dot(act, wout_ref[c * piece:(c + 1) * piece, :])
        acc = part if acc is None else acc + part
    y = x + 0.5 * acc
    if final_norm:
        y = _rmsnorm(y, gf_ref[...])
    o_ref[...] = y


def _ffn(x, l, g, w_in, w_out, g_final, final_norm):
    rows = x.shape[0]
    tm = min(ROW_TILE, rows)
    assert rows % tm == 0
    return pl.pallas_call(
        functools.partial(_ffn_body, final_norm=final_norm),
        grid=(rows // tm,),
        in_specs=[
            pl.BlockSpec((tm, D), lambda i: (i, 0)),
            _layer_spec(g, l),
            _layer_spec(w_in, l, pipeline_mode=pl.Buffered(1)),
            _layer_spec(w_out, l, pipeline_mode=pl.Buffered(1)),
            _const_spec((1, D)),
        ],
        out_specs=pl.BlockSpec((tm, D), lambda i: (i, 0)),
        out_shape=jax.ShapeDtypeStruct((rows, D), F32),
        compiler_params=pltpu.CompilerParams(
            dimension_semantics=("arbitrary",), vmem_limit_bytes=VMEM_LIMIT),
        name="ffn",
    )(x, g, w_in, w_out, g_final)


def _proj_columns(x_ref, g_ref, w_ref):
    xb = _rmsnorm(x_ref[...], g_ref[...]).astype(BF16)
    ps = _dot(xb, w_ref[:, 0:DS])
    pc = _dot(xb, w_ref[:, DS:DS + 3 * D])
    pg = _dot(xb, w_ref[:, DS + 3 * D:DP])
    return ps, pc, pg


def _proj_seq_body(x_ref, g_ref, w_ref, mu_ref, cw_ref, sh0_ref, cv0_ref,
                   xs_ref, mb_ref, sa_ref, sh_ref, cv_ref, shc_s, cvc_s, *, tm, tps):
    i = pl.program_id(0)

    @pl.when(i % tps == 0)
    def _():
        shc_s[...] = sh0_ref[0]
        cvc_s[...] = cv0_ref[0]

    ps, pc, pg = _proj_columns(x_ref, g_ref, w_ref)
    row = lax.broadcasted_iota(jnp.int32, (tm, 1), 0)
    prev = jnp.where(row == 0, shc_s[...], pltpu.roll(ps, 1, 0))
    xs_ref[...] = ps + (prev - ps) * mu_ref[...]
    last_ps = ps[tm - 1:tm, :]
    shc_s[...] = last_ps

    u = pc[:, D:2 * D] * pc[:, 2 * D:3 * D]
    c0 = cvc_s[0:1, :]
    c1 = cvc_s[1:2, :]
    u1 = jnp.where(row == 0, c1, pltpu.roll(u, 1, 0))
    u2 = jnp.where(row == 0, c0, jnp.where(row == 1, c1, pltpu.roll(u, 2, 0)))
    z = cw_ref[0:1, :] * u2 + cw_ref[1:2, :] * u1 + cw_ref[2:3, :] * u
    mb_ref[...] = _sigmoid(pg[:, D:2 * D]) * (pc[:, 0:D] * z)
    sa_ref[...] = _sigmoid(pg[:, 0:D])
    last_u = u[tm - 2:tm, :]
    cvc_s[...] = last_u

    @pl.when(i % tps == tps - 1)
    def _():
        sh_ref[0] = last_ps
        cv_ref[0] = last_u


def _proj_seq(x, seq_len, l, g, w, mu, cw, sh0, cv0):
    rows = x.shape[0]
    tm = min(PROJ_TILE, seq_len)
    assert rows % seq_len == 0 and seq_len % tm == 0 and tm >= 2
    nseq, tps = rows // seq_len, seq_len // tm
    return pl.pallas_call(
        functools.partial(_proj_seq_body, tm=tm, tps=tps),
        grid=(rows // tm,),
        in_specs=[
            pl.BlockSpec((tm, D), lambda i: (i, 0)),
            _layer_spec(g, l),
            _layer_spec(w, l, pipeline_mode=pl.Buffered(1)),
            _layer_spec(mu, l),
            _layer_spec(cw, l),
            _const_spec((1, 1, DS)),
            _const_spec((1, 2, D)),
        ],
        out_specs=[
            pl.BlockSpec((tm, DS), lambda i: (i, 0)),
            pl.BlockSpec((tm, D), lambda i: (i, 0)),
            pl.BlockSpec((tm, D), lambda i: (i, 0)),
            pl.BlockSpec((1, 1, DS), lambda i: (i // tps, 0, 0)),
            pl.BlockSpec((1, 2, D), lambda i: (i // tps, 0, 0)),
        ],
        out_shape=[
            jax.ShapeDtypeStruct((rows, DS), F32),
            jax.ShapeDtypeStruct((rows, D), F32),
            jax.ShapeDtypeStruct((rows, D), F32),
            jax.ShapeDtypeStruct((nseq, 1, DS), F32),
            jax.ShapeDtypeStruct((nseq, 2, D), F32),
        ],
        scratch_shapes=[pltpu.VMEM((1, DS), F32), pltpu.VMEM((2, D), F32)],
        compiler_params=pltpu.CompilerParams(
            dimension_semantics=("arbitrary",), vmem_limit_bytes=VMEM_LIMIT),
        name="proj_seq",
    )(x, g, w, mu, cw, sh0, cv0)


def _proj_step_body(x_ref, g_ref, w_ref, mu_ref, cw_ref, sh_ref, cv_ref,
                    xs_ref, mb_ref, sa_ref, sho_ref, cvo_ref):
    ps, pc, pg = _proj_columns(x_ref, g_ref, w_ref)
    xs_ref[...] = ps + (sh_ref[...] - ps) * mu_ref[...]
    sho_ref[...] = ps
    u = pc[:, D:2 * D] * pc[:, 2 * D:3 * D]
    c1 = cv_ref[:, 1, :]
    z = cw_ref[0:1, :] * cv_ref[:, 0, :] + cw_ref[1:2, :] * c1 + cw_ref[2:3, :] * u
    mb_ref[...] = _sigmoid(pg[:, D:2 * D]) * (pc[:, 0:D] * z)
    sa_ref[...] = _sigmoid(pg[:, 0:D])
    cvo_ref[:, 0, :] = c1
    cvo_ref[:, 1, :] = u


def _proj_step(x, l, g, w, mu, cw, sh_all, cv_all):
    rows = x.shape[0]
    return pl.pallas_call(
        _proj_step_body,
        grid=(1,),
        in_specs=[
            _const_spec((rows, D)),
            _layer_spec(g, l),
            _layer_spec(w, l, pipeline_mode=pl.Buffered(1)),
            _layer_spec(mu, l),
            _layer_spec(cw, l),
            _layer_spec(sh_all, l),
            _layer_spec(cv_all, l),
        ],
        out_specs=[
            _const_spec((rows, DS)),
            _const_spec((rows, D)),
            _const_spec((rows, D)),
            _const_spec((rows, DS)),
            _const_spec((rows, 2, D)),
        ],
        out_shape=[
            jax.ShapeDtypeStruct((rows, DS), F32),
            jax.ShapeDtypeStruct((rows, D), F32),
            jax.ShapeDtypeStruct((rows, D), F32),
            jax.ShapeDtypeStruct((rows, DS), F32),
            jax.ShapeDtypeStruct((rows, 2, D), F32),
        ],
        compiler_params=pltpu.CompilerParams(
            dimension_semantics=("arbitrary",), vmem_limit_bytes=VMEM_LIMIT),
        name="proj_step",
    )(x, g, w, mu, cw, sh_all, cv_all)


def _seg_sum(x, e):
    return jnp.concatenate([_dot_split(x[:, g * GL:(g + 1) * GL], e) for g in range(NG)], axis=1)


def _mix_heads(xs, w0_ref, w2_ref, a0_ref, a2_ref, wg2_ref, kk_ref, ka_ref, rk_ref, e):
    r = xs(0, D)
    k = xs(D, 2 * D)
    v = xs(2 * D, 3 * D)
    lwla = xs(3 * D, 3 * D + LORA_WA)
    lg = xs(3 * D + LORA_WA, DS)
    wl = w0_ref[...] + _dot(jnp.tanh(lwla).astype(BF16), w2_ref[...])
    dexp = EXP_M05 * _sigmoid(wl)
    alr = _sigmoid(a0_ref[...] + _dot(lwla.astype(BF16), a2_ref[...]))
    gate = _dot(_sigmoid(lg).astype(BF16), wg2_ref[...])
    kkr = k * kk_ref[...]
    kk = kkr * lax.rsqrt(jnp.maximum(_seg_sum(kkr * kkr, e), 1e-24))
    kh = k * (1.0 + (alr - 1.0) * ka_ref[...])
    bonus = _seg_sum(r * kh * rk_ref[...], e) * v
    return r, kh, v, kk, alr, dexp, gate, bonus


def _mix_out(y, bonus, gate, sa, mb, x, lnw_ref, lnb_ref, wo_ref, e):
    mu = _seg_sum(y, e) * (1.0 / HD)
    d = y - mu
    var = _seg_sum(d * d, e) * (1.0 / HD)
    yn = d * lax.rsqrt(var + GN_EPS) * lnw_ref[...] + lnb_ref[...]
    m = sa * ((yn + bonus) * gate) + mb
    return x + _dot(m.astype(BF16), wo_ref[...])


def _mix_seq_body(xs_ref, mb_ref, sa_ref, x_ref, s0_ref,
                  w0_ref, w2_ref, a0_ref, a2_ref, wg2_ref, kk_ref, ka_ref, rk_ref,
                  lnw_ref, lnb_ref, wo_ref, e_ref,
                  o_ref, so_ref,
                  st_s, ah_s, rh_s, bt_s, kt_s, v_s, wl_s, y_s, *, tl, c):
    j = pl.program_id(1)
    gc = G * c
    nchunk = tl // c

    @pl.when(j == 0)
    def _():
        st_s[...] = jnp.zeros_like(st_s)
        for h in range(H):
            d0 = (h % G) * HD
            st_s[h // G, d0:d0 + HD, d0:d0 + HD] = s0_ref[0, h]

    e = e_ref[...]
    row = lax.broadcasted_iota(jnp.int32, (tl, 1), 0)
    r, kh, v, kk, alr, dexp, gate, bonus = _mix_heads(
        lambda a, b: xs_ref[0, :, a:b],
        w0_ref, w2_ref, a0_ref, a2_ref, wg2_ref, kk_ref, ka_ref, rk_ref, e)

    rin = row % c
    cum = dexp
    s = 1
    while s < c:
        cum = cum + jnp.where(rin >= s, pltpu.roll(cum, s, 0), 0.0)
        s *= 2
    wc = jnp.exp(-cum)
    winv = jnp.exp(cum)
    rh_s[...] = (r * wc).astype(BF16)
    ah_s[...] = (-kk * jnp.exp(dexp - cum)).astype(BF16)
    bt_s[...] = (kk * alr * winv).astype(BF16)
    kt_s[...] = (kh * winv).astype(BF16)
    v_s[...] = v.astype(BF16)
    for ci in range(nchunk):
        wl_s[ci:ci + 1, :] = wc[(ci + 1) * c - 1:(ci + 1) * c, :]

    groups = range(NG)
    lanes = [slice(g * GL, (g + 1) * GL) for g in groups]

    ri = lax.broadcasted_iota(jnp.int32, (gc, gc), 0)
    cj = lax.broadcasted_iota(jnp.int32, (gc, gc), 1)
    same = (ri // c) == (cj // c)
    strict = same & ((ri % c) > (cj % c))
    incl = same & ((ri % c) >= (cj % c))
    eye = (ri == cj).astype(F32)
    smask = (lax.broadcasted_iota(jnp.int32, (gc, GL), 0) // c) == (
        lax.broadcasted_iota(jnp.int32, (gc, GL), 1) // HD)

    def stack(x):
        return jnp.where(smask, jnp.concatenate([x] * G, axis=0), jnp.zeros((), x.dtype))

    def nt(a, b):
        return lax.dot_general(a, b, _NT, preferred_element_type=F32)

    def tn(a, b):
        return lax.dot_general(a, b, _TN, preferred_element_type=F32)

    merged = gc % LANES == 0 and 2 * gc <= MXU_DIM

    def state_free_stages(q):
        rs = slice(q * c, (q + 1) * c)
        t = {"rs": rs}

        def s_products_a():
            t["ah"] = [ah_s[rs, ls] for ls in lanes]
            t["bt"] = [bt_s[rs, ls] for ls in lanes]
            t["kt"] = [kt_s[rs, ls] for ls in lanes]
            a_st = [stack(x) for x in t["ah"]]
            b_rep = [jnp.concatenate([x] * G, axis=0) for x in t["bt"]]
            k_rep = [jnp.concatenate([x] * G, axis=0) for x in t["kt"]]
            if merged:
                t["rh"] = [rh_s[rs, ls] for ls in lanes]
                ar_st = [jnp.concatenate([a_st[g], stack(t["rh"][g])], axis=0) for g in groups]
                bk = [jnp.concatenate([b_rep[g], k_rep[g]], axis=0) for g in groups]
                gar = [nt(ar_st[g], bk[g]) for g in groups]
                l_ab = [jnp.where(strict, gar[g][:gc, :gc], 0.0) for g in groups]
                t["a_ak"] = [jnp.where(strict, gar[g][:gc, gc:], 0.0).astype(BF16) for g in groups]
                t["a_r"] = [jnp.concatenate([jnp.where(incl, gar[g][gc:, :gc], 0.0),
                                             jnp.where(incl, gar[g][gc:, gc:], 0.0)], axis=1).astype(BF16)
                            for g in groups]
            else:
                t["b_rep"], t["k_rep"] = b_rep, k_rep
                l_ab = [jnp.where(strict, nt(a_st[g], b_rep[g]), 0.0) for g in groups]
                t["a_ak"] = [jnp.where(strict, nt(a_st[g], k_rep[g]), 0.0).astype(BF16) for g in groups]
            t["lp"] = [x.astype(BF16) for x in l_ab]
            t["p"] = [eye + x for x in l_ab]

        def s_products_r():
            if not merged:
                t["rh"] = [rh_s[rs, ls] for ls in lanes]
                r_st = [stack(x) for x in t["rh"]]
                t["a_rb"] = [jnp.where(incl, nt(r_st[g], t["b_rep"][g]), 0.0).astype(BF16) for g in groups]
                t["a_rk"] = [jnp.where(incl, nt(r_st[g], t["k_rep"][g]), 0.0).astype(BF16) for g in groups]

        def s_av():
            t["v_st"] = [stack(v_s[rs, ls]) for ls in lanes]
            t["av"] = [_dot(t["a_ak"][g], t["v_st"][g]) for g in groups]

        def s_square():
            t["lp"] = [_dot(x, x).astype(BF16) for x in t["lp"]]

        def s_extend():
            t["p"] = [t["p"][g] + _dot(t["lp"][g], t["p"][g].astype(BF16)) for g in groups]

        def s_square_extend():
            both = [_dot(t["lp"][g], jnp.concatenate([t["lp"][g], t["p"][g].astype(BF16)], axis=1))
                    for g in groups]
            t["lp"] = [x[:, :gc].astype(BF16) for x in both]
            t["p"] = [t["p"][g] + both[g][:, gc:] for g in groups]

        def s_finish():
            t["pb"] = [x.astype(BF16) for x in t["p"]]
            t["b_st"] = [stack(x) for x in t["bt"]]
            t["k_st"] = [stack(x) for x in t["kt"]]

        stages = [s_products_a, s_products_r, s_av]
        factors = c.bit_length() - 1
        if merged:
            stages += [s_square] + [s_square_extend] * (factors - 2) + [s_extend]
        else:
            stages += [s_square, s_extend] * (factors - 1)
        return t, stages + [s_finish]

    def state_chain_stages(q, t, st):
        rs = t["rs"]
        w = {}

        def s_read_state():
            w["ars"] = [nt(jnp.concatenate([t["ah"][g], t["rh"][g]], axis=0), st[g].astype(BF16))
                        for g in groups]

        def s_solve():
            x_st = [stack(w["ars"][g][:c]) + t["av"][g] for g in groups]
            w["u"] = [_dot(t["pb"][g], x_st[g].astype(BF16)).astype(BF16) for g in groups]

        def s_outputs():
            if merged:
                uv = [jnp.concatenate([w["u"][g], t["v_st"][g]], axis=0) for g in groups]
                w["y"] = [_dot(t["a_r"][g], uv[g]) for g in groups]
                w["upd"] = [tn(uv[g], jnp.concatenate([t["b_st"][g], t["k_st"][g]], axis=0))
                            for g in groups]
            else:
                w["y"] = [_dot(t["a_rb"][g], w["u"][g]) + _dot(t["a_rk"][g], t["v_st"][g]) for g in groups]
                w["upd"] = [tn(w["u"][g], t["b_st"][g]) + tn(t["v_st"][g], t["k_st"][g]) for g in groups]

        def s_commit():
            for g in groups:
                y = w["ars"][g][c:]
                for hh in range(G):
                    y = y + w["y"][g][hh * c:(hh + 1) * c]
                y_s[rs, lanes[g]] = y
                st[g] = (st[g] + w["upd"][g]) * wl_s[q:q + 1, lanes[g]]

        return [s_read_state, s_solve, s_outputs, s_commit]

    st = [st_s[g] for g in groups]
    cur_t, first = state_free_stages(0)
    for stage in first:
        stage()
    for q in range(nchunk):
        chain = state_chain_stages(q, cur_t, st)
        nxt_t, free = state_free_stages(q + 1) if q + 1 < nchunk else (None, [])
        every = max(1, len(free) // len(chain))
        fi = 0
        for stage in chain:
            for _ in range(every):
                if fi < len(free):
                    free[fi]()
                    fi += 1
            stage()
        while fi < len(free):
            free[fi]()
            fi += 1
        cur_t = nxt_t
    for g in groups:
        st_s[g] = st[g]

    o_ref[0] = _mix_out(y_s[...], bonus, gate, sa_ref[0], mb_ref[0], x_ref[0],
                        lnw_ref, lnb_ref, wo_ref, e)

    @pl.when(j == pl.num_programs(1) - 1)
    def _():
        for h in range(H):
            d0 = (h % G) * HD
            so_ref[0, h] = st_s[h // G, d0:d0 + HD, d0:d0 + HD]


def _mix_seq(xs, mb, sa, x, s0, l, wts, e, *, tl, c):
    bsz, length, _ = x.shape
    assert length % tl == 0 and tl % c == 0
    nchunk = tl // c

    def tile(width):
        return pl.BlockSpec((1, tl, width), lambda b, j: (b, j, 0))

    return pl.pallas_call(
        functools.partial(_mix_seq_body, tl=tl, c=c),
        grid=(bsz, length // tl),
        in_specs=[tile(DS), tile(D), tile(D), tile(D), _const_spec((1, H, HD, HD))]
        + [_layer_spec(w, l) for w in wts] + [_const_spec(e.shape)],
        out_specs=[tile(D), pl.BlockSpec((1, H, HD, HD), lambda b, j: (b, 0, 0, 0))],
        out_shape=[
            jax.ShapeDtypeStruct((bsz, length, D), F32),
            jax.ShapeDtypeStruct((bsz, H, HD, HD), F32),
        ],
        scratch_shapes=[pltpu.VMEM((NG, GL, GL), F32)] + [pltpu.VMEM((tl, D), BF16)] * 5 + [
            pltpu.VMEM((max(8, nchunk), D), F32),
            pltpu.VMEM((tl, D), F32),
        ],
        compiler_params=pltpu.CompilerParams(
            dimension_semantics=("arbitrary", "arbitrary"), vmem_limit_bytes=VMEM_LIMIT),
        name=f"mix_seq_{tl}",
    )(xs, mb, sa, x, s0, *wts, e)


def _mix_step_body(xs_ref, mb_ref, sa_ref, x_ref, s_ref,
                   w0_ref, w2_ref, a0_ref, a2_ref, wg2_ref, kk_ref, ka_ref, rk_ref,
                   lnw_ref, lnb_ref, wo_ref, e_ref,
                   o_ref, so_ref,
                   a_s, wr_s, w_s, b_s, k_s, v_s, s1_s, s2_s, z_s, *, bt):
    e = e_ref[...]
    r, kh, v, kk, alr, dexp, gate, bonus = _mix_heads(
        lambda a, b: xs_ref[:, a:b],
        w0_ref, w2_ref, a0_ref, a2_ref, wg2_ref, kk_ref, ka_ref, rk_ref, e)
    w = jnp.exp(-dexp)
    bb = kk * alr
    lane = lax.broadcasted_iota(jnp.int32, (bt, LANES), 1)

    def fill(dst, val):
        for p in range(H // 2):
            blk = val[:, p * LANES:(p + 1) * LANES]
            rolled = pltpu.roll(blk, HD, 1)
            dst[2 * p] = jnp.where(lane < HD, blk, rolled)
            dst[2 * p + 1] = jnp.where(lane < HD, rolled, blk)

    fill(a_s, -kk)
    fill(wr_s, w * r)
    fill(w_s, w)
    fill(b_s, bb)
    fill(k_s, kh)
    fill(v_s, v)
    fill(s1_s, _seg_sum(bb * r, e))
    fill(s2_s, _seg_sum(kh * r, e))

    li = lax.broadcasted_iota(jnp.int32, (LANES, LANES), 0)
    lj = lax.broadcasted_iota(jnp.int32, (LANES, LANES), 1)
    half_sum = ((li // HD) == (lj // HD)).astype(BF16)
    half_sum2 = jnp.concatenate([half_sum, half_sum], axis=0)
    src_v = 2 * (li % HD) + li // HD
    perm_lo = ((li % HD < PK) & (lj == src_v)).astype(BF16)
    perm_hi = ((li % HD < PK) & (lj == src_v + HD)).astype(BF16)
    pi = lax.broadcasted_iota(jnp.int32, (PK, LANES), 0)
    pj = lax.broadcasted_iota(jnp.int32, (PK, LANES), 1)
    pick_v = ((pj % HD) == 2 * pi + pj // HD).astype(F32)
    diag = ((pj % HD) == pi).astype(F32)

    def seq_step(b, carry):
        rb = pl.ds(b, 1)
        xa, xw, xv = [], [], []
        for h in range(H):
            s = s_ref[b, h]
            xa.append((s * a_s[h, rb, :]).astype(BF16))
            xw.append((s * wr_s[h, rb, :]).astype(BF16))
            vm = pick_v * v_s[h, rb, :]
            vhi = vm.astype(BF16)
            xv.append(jnp.concatenate([vhi, (vm - vhi.astype(F32)).astype(BF16)], axis=1))
        sa = _dot(jnp.concatenate(xa, axis=0), half_sum)
        swr = _dot(jnp.concatenate(xw, axis=0), half_sum)
        vb = _dot(jnp.concatenate(xv, axis=0), half_sum2)
        for h in range(H):
            hs = slice(h * PK, (h + 1) * PK)
            so_ref[b, h] = (s_ref[b, h] * w_s[h, rb, :] + sa[hs] * b_s[h, rb, :]
                            + vb[hs] * k_s[h, rb, :])
            y = swr[hs] + sa[hs] * s1_s[h, rb, :] + vb[hs] * s2_s[h, rb, :]
            z_s[h, rb, :] = jnp.sum(y * diag, axis=0, keepdims=True)
        return carry

    lax.fori_loop(0, bt, seq_step, 0)
    y = jnp.concatenate(
        [_dot_split(z_s[2 * p], perm_lo) + _dot_split(z_s[2 * p + 1], perm_hi) for p in range(H // 2)],
        axis=1)
    o_ref[...] = _mix_out(y, bonus, gate, sa_ref[...], mb_ref[...], x_ref[...],
                          lnw_ref, lnb_ref, wo_ref, e)


def _mix_step(xs, mb, sa, x, states, l, wts, e):
    bsz = x.shape[0]
    bt = STEP_TILE
    assert bsz % bt == 0
    state_spec = pl.BlockSpec((None, bt, H, PK, LANES), lambda i: (l, i, 0, 0, 0))

    def rows(width):
        return pl.BlockSpec((bt, width), lambda i: (i, 0))

    tokens = (xs, mb, sa, x)
    return pl.pallas_call(
        functools.partial(_mix_step_body, bt=bt),
        grid=(bsz // bt,),
        in_specs=[rows(DS), rows(D), rows(D), rows(D), state_spec]
        + [_layer_spec(w, l) for w in wts] + [_const_spec(e.shape)],
        out_specs=[rows(D), state_spec],
        out_shape=[
            jax.ShapeDtypeStruct((bsz, D), F32),
            jax.ShapeDtypeStruct(states.shape, F32),
        ],
        scratch_shapes=[pltpu.VMEM((H, bt, LANES), F32)] * 9,
        input_output_aliases={len(tokens): 1},
        compiler_params=pltpu.CompilerParams(
            dimension_semantics=("arbitrary",), vmem_limit_bytes=VMEM_LIMIT),
        name="mix_step",
    )(*tokens, states, *wts, e)


def kernel(x_prompt, x_sample, state_wkv, state_shift, state_conv, meta_tokens, norm_ffn1, ffn1_w_in, ffn1_w_out, norm_mix, w_in, mu_shift, w0, w_w2, a0, w_a2, w_g2, k_k, k_a, r_k, lnx_w, lnx_b, conv_w, w_o, norm_ffn2, ffn2_w_in, ffn2_w_out, norm_final):
    depth = norm_ffn1.shape[0]
    bp, lp, _ = x_prompt.shape
    bs = x_sample.shape[0]
    n_meta = meta_tokens.shape[0]

    def vec(a):
        return a.reshape(depth, 1, -1)

    half_zeros = jnp.zeros((depth, LORA_WA // 2, D), F32)
    wts = tuple(dict(
        w0=vec(w0),
        w2=jnp.concatenate([w_w2, half_zeros], axis=1).astype(BF16),
        a0=vec(a0),
        a2=jnp.concatenate([half_zeros, w_a2], axis=1).astype(BF16),
        wg2=w_g2.astype(BF16), k_k=vec(k_k), k_a=vec(k_a), r_k=vec(r_k), lnw=vec(lnx_w),
        lnb=vec(lnx_b), wo=w_o.astype(BF16))[name] for name in _MIX_WEIGHTS)
    lane = jnp.arange(GL)
    e = (lane[:, None] // HD == lane[None, :] // HD).astype(BF16)
    n1, nm, n2, mu = vec(norm_ffn1), vec(norm_mix), vec(norm_ffn2), vec(mu_shift)
    f1_in, f1_out = ffn1_w_in.astype(BF16), ffn1_w_out.astype(BF16)
    f2_in, f2_out = ffn2_w_in.astype(BF16), ffn2_w_out.astype(BF16)
    w_in_b = w_in.astype(BF16)
    gfin = norm_final[None]

    xm = meta_tokens.astype(x_prompt.dtype)
    xp = x_prompt.reshape(bp * lp, D)
    xs = x_sample.reshape(bs, D)
    states = state_wkv.reshape(depth, bs, H, PK, LANES)
    zero_s = jnp.zeros((1, H, HD, HD), F32)
    zero_sh = jnp.zeros((1, 1, DS), F32)
    zero_cv = jnp.zeros((1, 2, D), F32)
    p_wkv, p_shift, p_conv, s_shift, s_conv = [], [], [], [], []
    for l in range(depth):
        xm = _ffn(xm, l, n1, f1_in, f1_out, gfin, False)
        xp = _ffn(xp, l, n1, f1_in, f1_out, gfin, False)
        xs = _ffn(xs, l, n1, f1_in, f1_out, gfin, False)
        m_xs, m_mb, m_sa, m_sh, m_cv = _proj_seq(xm, n_meta, l, nm, w_in_b, mu, conv_w, zero_sh, zero_cv)
        p_xs, p_mb, p_sa, p_sh, p_cv = _proj_seq(xp, lp, l, nm, w_in_b, mu, conv_w, m_sh, m_cv)
        s_xs, s_mb, s_sa, s_sh, s_cv = _proj_step(xs, l, nm, w_in_b, mu, conv_w, state_shift, state_conv)
        xm3, m_s = _mix_seq(m_xs.reshape(1, n_meta, DS), m_mb.reshape(1, n_meta, D),
                            m_sa.reshape(1, n_meta, D), xm.reshape(1, n_meta, D), zero_s,
                            l, wts, e, tl=n_meta, c=n_meta)
        xp3, p_s = _mix_seq(p_xs.reshape(bp, lp, DS), p_mb.reshape(bp, lp, D), p_sa.reshape(bp, lp, D),
                            xp.reshape(bp, lp, D), m_s, l, wts, e, tl=min(SEQ_TILE, lp), c=CHUNK)
        xs, states = _mix_step(s_xs, s_mb, s_sa, xs, states, l, wts, e)
        xm = xm3.reshape(n_meta, D)
        xp = xp3.reshape(bp * lp, D)
        last = l == depth - 1
        if not last:
            xm = _ffn(xm, l, n2, f2_in, f2_out, gfin, False)
        xp = _ffn(xp, l, n2, f2_in, f2_out, gfin, last)
        xs = _ffn(xs, l, n2, f2_in, f2_out, gfin, last)
        p_wkv.append(p_s)
        p_shift.append(p_sh[:, 0])
        p_conv.append(p_cv)
        s_shift.append(s_sh)
        s_conv.append(s_cv)
    return (xp.reshape(bp, lp, D), xs.reshape(bs, 1, D), jnp.stack(p_wkv), jnp.stack(p_shift),
            jnp.stack(p_conv), states.reshape(state_wkv.shape), jnp.stack(s_shift), jnp.stack(s_conv))
```

```python
import functools
import math

import jax
import jax.numpy as jnp
from jax import lax
from jax.experimental import pallas as pl
from jax.experimental.pallas import tpu as pltpu

F32 = jnp.float32
BF16 = jnp.bfloat16

D = 1024
H = 16
HD = 64
G = 4
GL = G * HD
NG = H // G
LORA_WA = 128
LORA_G = 128
DS = 3 * D + LORA_WA + LORA_G
DP = DS + 3 * D + 2 * D
DFF = 2816
RMS_EPS = 1e-6
GN_EPS = 64e-5
EXP_M05 = math.exp(-0.5)
LANES = 128
MXU_DIM = 256
PK = HD * HD // LANES

CHUNK = 32
SEQ_TILE = 256
ROW_TILE = 512
PROJ_TILE = 256
STEP_TILE = 32
FF_SPLIT = 2
VMEM_LIMIT = 56 * 1024 * 1024

_NT = (((1,), (1,)), ((), ()))
_TN = (((0,), (0,)), ((), ()))

_MIX_WEIGHTS = ("w0", "w2", "a0", "a2", "wg2", "k_k", "k_a", "r_k", "lnw", "lnb", "wo")


def _dot(a, b):
    return jnp.dot(a, b, preferred_element_type=F32)


def _dot_split(x, m):
    hi = x.astype(BF16)
    lo = (x - hi.astype(F32)).astype(BF16)
    return _dot(hi, m) + _dot(lo, m)


def _rmsnorm(x, g):
    return x * lax.rsqrt(jnp.mean(x * x, axis=-1, keepdims=True) + RMS_EPS) * g


def _sigmoid(x):
    return 1.0 / (1.0 + jnp.exp(-x))


def _const_spec(shape):
    nd = len(shape)
    return pl.BlockSpec(shape, lambda *_: (0,) * nd)


def _layer_spec(arr, l, **kw):
    nd = arr.ndim - 1
    return pl.BlockSpec((None,) + arr.shape[1:], lambda *_: (l,) + (0,) * nd, **kw)


def _ffn_body(x_ref, g_ref, win_ref, wout_ref, gf_ref, o_ref, *, final_norm):
    x = x_ref[...]
    xb = _rmsnorm(x, g_ref[...]).astype(BF16)
    piece = DFF // FF_SPLIT
    acc = None
    for c in range(FF_SPLIT):
        gate = _dot(xb, win_ref[:, c * piece:(c + 1) * piece])
        up = _dot(xb, win_ref[:, DFF + c * piece:DFF + (c + 1) * piece])
        act = (gate * _sigmoid(gate) * up).astype(BF16)
        part = _dot(act, wout_ref[c * piece:(c + 1) * piece, :])
        acc = part if acc is None else acc + part
    y = x + 0.5 * acc
    if final_norm:
        y = _rmsnorm(y, gf_ref[...])
    o_ref[...] = y


def _ffn(x, l, g, w_in, w_out, g_final, final_norm):
    rows = x.shape[0]
    tm = min(ROW_TILE, rows)
    assert rows % tm == 0
    return pl.pallas_call(
        functools.partial(_ffn_body, final_norm=final_norm),
        grid=(rows // tm,),
        in_specs=[
            pl.BlockSpec((tm, D), lambda i: (i, 0)),
            _layer_spec(g, l),
            _layer_spec(w_in, l, pipeline_mode=pl.Buffered(1)),
            _layer_spec(w_out, l, pipeline_mode=pl.Buffered(1)),
            _const_spec((1, D)),
        ],
        out_specs=pl.BlockSpec((tm, D), lambda i: (i, 0)),
        out_shape=jax.ShapeDtypeStruct((rows, D), F32),
        compiler_params=pltpu.CompilerParams(
            dimension_semantics=("arbitrary",), vmem_limit_bytes=VMEM_LIMIT),
        name="ffn",
    )(x, g, w_in, w_out, g_final)


def _proj_columns(x_ref, g_ref, w_ref):
    xb = _rmsnorm(x_ref[...], g_ref[...]).astype(BF16)
    ps = _dot(xb, w_ref[:, 0:DS])
    pc = _dot(xb, w_ref[:, DS:DS + 3 * D])
    pg = _dot(xb, w_ref[:, DS + 3 * D:DP])
    return ps, pc, pg


def _proj_seq_body(x_ref, g_ref, w_ref, mu_ref, cw_ref, sh0_ref, cv0_ref,
                   xs_ref, mb_ref, sa_ref, sh_ref, cv_ref, shc_s, cvc_s, *, tm, tps):
    i = pl.program_id(0)

    @pl.when(i % tps == 0)
    def _():
        shc_s[...] = sh0_ref[0]
        cvc_s[...] = cv0_ref[0]

    xb = _rmsnorm(x_ref[...], g_ref[...]).astype(BF16)
    row = lax.broadcasted_iota(jnp.int32, (tm, 1), 0)

    pc = _dot(xb, w_ref[:, DS:DS + 3 * D])
    u = pc[:, D:2 * D] * pc[:, 2 * D:3 * D]
    c0 = cvc_s[0:1, :]
    c1 = cvc_s[1:2, :]
    u1 = jnp.where(row == 0, c1, pltpu.roll(u, 1, 0))
    u2 = jnp.where(row == 0, c0, jnp.where(row == 1, c1, pltpu.roll(u, 2, 0)))
    z = cw_ref[0:1, :] * u2 + cw_ref[1:2, :] * u1 + cw_ref[2:3, :] * u
    pg = _dot(xb, w_ref[:, DS + 3 * D:DP])
    mb_ref[...] = _sigmoid(pg[:, D:2 * D]) * (pc[:, 0:D] * z)
    sa_ref[...] = _sigmoid(pg[:, 0:D])
    last_u = u[tm - 2:tm, :]
    cvc_s[...] = last_u

    ps = _dot(xb, w_ref[:, 0:DS])
    prev = jnp.where(row == 0, shc_s[...], pltpu.roll(ps, 1, 0))
    xs_ref[...] = ps + (prev - ps) * mu_ref[...]
    last_ps = ps[tm - 1:tm, :]
    shc_s[...] = last_ps

    @pl.when(i % tps == tps - 1)
    def _():
        sh_ref[0] = last_ps
        cv_ref[0] = last_u


def _proj_seq(x, seq_len, l, g, w, mu, cw, sh0, cv0):
    rows = x.shape[0]
    tm = min(PROJ_TILE, seq_len)
    assert rows % seq_len == 0 and seq_len % tm == 0 and tm >= 2
    nseq, tps = rows // seq_len, seq_len // tm
    return pl.pallas_call(
        functools.partial(_proj_seq_body, tm=tm, tps=tps),
        grid=(rows // tm,),
        in_specs=[
            pl.BlockSpec((tm, D), lambda i: (i, 0)),
            _layer_spec(g, l),
            _layer_spec(w, l, pipeline_mode=pl.Buffered(1)),
            _layer_spec(mu, l),
            _layer_spec(cw, l),
            _const_spec((1, 1, DS)),
            _const_spec((1, 2, D)),
        ],
        out_specs=[
            pl.BlockSpec((tm, DS), lambda i: (i, 0)),
            pl.BlockSpec((tm, D), lambda i: (i, 0)),
            pl.BlockSpec((tm, D), lambda i: (i, 0)),
            pl.BlockSpec((1, 1, DS), lambda i: (i // tps, 0, 0)),
            pl.BlockSpec((1, 2, D), lambda i: (i // tps, 0, 0)),
        ],
        out_shape=[
            jax.ShapeDtypeStruct((rows, DS), F32),
            jax.ShapeDtypeStruct((rows, D), F32),
            jax.ShapeDtypeStruct((rows, D), F32),
            jax.ShapeDtypeStruct((nseq, 1, DS), F32),
            jax.ShapeDtypeStruct((nseq, 2, D), F32),
        ],
        scratch_shapes=[pltpu.VMEM((1, DS), F32), pltpu.VMEM((2, D), F32)],
        compiler_params=pltpu.CompilerParams(
            dimension_semantics=("arbitrary",), vmem_limit_bytes=VMEM_LIMIT),
        name="proj_seq",
    )(x, g, w, mu, cw, sh0, cv0)


def _proj_step_body(x_ref, g_ref, w_ref, mu_ref, cw_ref, sh_ref, cv_ref,
                    xs_ref, mb_ref, sa_ref, sho_ref, cvo_ref):
    ps, pc, pg = _proj_columns(x_ref, g_ref, w_ref)
    xs_ref[...] = ps + (sh_ref[...] - ps) * mu_ref[...]
    sho_ref[...] = ps
    u = pc[:, D:2 * D] * pc[:, 2 * D:3 * D]
    c1 = cv_ref[:, 1, :]
    z = cw_ref[0:1, :] * cv_ref[:, 0, :] + cw_ref[1:2, :] * c1 + cw_ref[2:3, :] * u
    mb_ref[...] = _sigmoid(pg[:, D:2 * D]) * (pc[:, 0:D] * z)
    sa_ref[...] = _sigmoid(pg[:, 0:D])
    cvo_ref[:, 0, :] = c1
    cvo_ref[:, 1, :] = u


def _proj_step(x, l, g, w, mu, cw, sh_all, cv_all):
    rows = x.shape[0]
    return pl.pallas_call(
        _proj_step_body,
        grid=(1,),
        in_specs=[
            _const_spec((rows, D)),
            _layer_spec(g, l),
            _layer_spec(w, l, pipeline_mode=pl.Buffered(1)),
            _layer_spec(mu, l),
            _layer_spec(cw, l),
            _layer_spec(sh_all, l),
            _layer_spec(cv_all, l),
        ],
        out_specs=[
            _const_spec((rows, DS)),
            _const_spec((rows, D)),
            _const_spec((rows, D)),
            _const_spec((rows, DS)),
            _const_spec((rows, 2, D)),
        ],
        out_shape=[
            jax.ShapeDtypeStruct((rows, DS), F32),
            jax.ShapeDtypeStruct((rows, D), F32),
            jax.ShapeDtypeStruct((rows, D), F32),
            jax.ShapeDtypeStruct((rows, DS), F32),
            jax.ShapeDtypeStruct((rows, 2, D), F32),
        ],
        compiler_params=pltpu.CompilerParams(
            dimension_semantics=("arbitrary",), vmem_limit_bytes=VMEM_LIMIT),
        name="proj_step",
    )(x, g, w, mu, cw, sh_all, cv_all)


def _seg_sum(x, e):
    return jnp.concatenate([_dot_split(x[:, g * GL:(g + 1) * GL], e) for g in range(NG)], axis=1)


def _mix_heads(xs, w0_ref, w2_ref, a0_ref, a2_ref, wg2_ref, kk_ref, ka_ref, rk_ref, e):
    r = xs(0, D)
    k = xs(D, 2 * D)
    v = xs(2 * D, 3 * D)
    lwla = xs(3 * D, 3 * D + LORA_WA)
    lg = xs(3 * D + LORA_WA, DS)
    wl = w0_ref[...] + _dot(jnp.tanh(lwla).astype(BF16), w2_ref[...])
    dexp = EXP_M05 * _sigmoid(wl)
    alr = _sigmoid(a0_ref[...] + _dot(lwla.astype(BF16), a2_ref[...]))
    gate = _dot(_sigmoid(lg).astype(BF16), wg2_ref[...])
    kkr = k * kk_ref[...]
    kk = kkr * lax.rsqrt(jnp.maximum(_seg_sum(kkr * kkr, e), 1e-24))
    kh = k * (1.0 + (alr - 1.0) * ka_ref[...])
    bonus = _seg_sum(r * kh * rk_ref[...], e) * v
    return r, kh, v, kk, alr, dexp, gate, bonus


def _mix_out(y, bonus, gate, sa, mb, x, lnw_ref, lnb_ref, wo_ref, e):
    mu = _seg_sum(y, e) * (1.0 / HD)
    d = y - mu
    var = _seg_sum(d * d, e) * (1.0 / HD)
    yn = d * lax.rsqrt(var + GN_EPS) * lnw_ref[...] + lnb_ref[...]
    m = sa * ((yn + bonus) * gate) + mb
    return x + _dot(m.astype(BF16), wo_ref[...])


def _mix_seq_body(xs_ref, mb_ref, sa_ref, x_ref, s0_ref,
                  w0_ref, w2_ref, a0_ref, a2_ref, wg2_ref, kk_ref, ka_ref, rk_ref,
                  lnw_ref, lnb_ref, wo_ref, e_ref,
                  o_ref, so_ref,
                  st_s, ah_s, rh_s, bt_s, kt_s, v_s, wl_s, y_s, *, tl, c):
    j = pl.program_id(1)
    gc = G * c
    nchunk = tl // c

    @pl.when(j == 0)
    def _():
        st_s[...] = jnp.zeros_like(st_s)
        for h in range(H):
            d0 = (h % G) * HD
            st_s[h // G, d0:d0 + HD, d0:d0 + HD] = s0_ref[0, h]

    e = e_ref[...]
    row = lax.broadcasted_iota(jnp.int32, (tl, 1), 0)
    r, kh, v, kk, alr, dexp, gate, bonus = _mix_heads(
        lambda a, b: xs_ref[0, :, a:b],
        w0_ref, w2_ref, a0_ref, a2_ref, wg2_ref, kk_ref, ka_ref, rk_ref, e)

    rin = row % c
    cum = dexp
    s = 1
    while s < c:
        cum = cum + jnp.where(rin >= s, pltpu.roll(cum, s, 0), 0.0)
        s *= 2
    wc = jnp.exp(-cum)
    winv = jnp.exp(cum)
    rh_s[...] = (r * wc).astype(BF16)
    ah_s[...] = (-kk * jnp.exp(dexp - cum)).astype(BF16)
    bt_s[...] = (kk * alr * winv).astype(BF16)
    kt_s[...] = (kh * winv).astype(BF16)
    v_s[...] = v.astype(BF16)
    for ci in range(nchunk):
        wl_s[ci:ci + 1, :] = wc[(ci + 1) * c - 1:(ci + 1) * c, :]

    groups = range(NG)
    lanes = [slice(g * GL, (g + 1) * GL) for g in groups]

    ri = lax.broadcasted_iota(jnp.int32, (gc, gc), 0)
    cj = lax.broadcasted_iota(jnp.int32, (gc, gc), 1)
    same = (ri // c) == (cj // c)
    strict = same & ((ri % c) > (cj % c))
    incl = same & ((ri % c) >= (cj % c))
    eye = (ri == cj).astype(F32)
    smask = (lax.broadcasted_iota(jnp.int32, (gc, GL), 0) // c) == (
        lax.broadcasted_iota(jnp.int32, (gc, GL), 1) // HD)

    def stack(x):
        return jnp.where(smask, jnp.concatenate([x] * G, axis=0), jnp.zeros((), x.dtype))

    def nt(a, b):
        return lax.dot_general(a, b, _NT, preferred_element_type=F32)

    def tn(a, b):
        return lax.dot_general(a, b, _TN, preferred_element_type=F32)

    merged = gc % LANES == 0 and 2 * gc <= MXU_DIM

    def state_free_stages(q):
        rs = slice(q * c, (q + 1) * c)
        t = {"rs": rs}

        def s_products_a():
            t["ah"] = [ah_s[rs, ls] for ls in lanes]
            t["bt"] = [bt_s[rs, ls] for ls in lanes]
            t["kt"] = [kt_s[rs, ls] for ls in lanes]
            a_st = [stack(x) for x in t["ah"]]
            b_rep = [jnp.concatenate([x] * G, axis=0) for x in t["bt"]]
            k_rep = [jnp.concatenate([x] * G, axis=0) for x in t["kt"]]
            if merged:
                t["rh"] = [rh_s[rs, ls] for ls in lanes]
                ar_st = [jnp.concatenate([a_st[g], stack(t["rh"][g])], axis=0) for g in groups]
                bk = [jnp.concatenate([b_rep[g], k_rep[g]], axis=0) for g in groups]
                gar = [nt(ar_st[g], bk[g]) for g in groups]
                l_ab = [jnp.where(strict, gar[g][:gc, :gc], 0.0) for g in groups]
                t["a_ak"] = [jnp.where(strict, gar[g][:gc, gc:], 0.0).astype(BF16) for g in groups]
                t["a_r"] = [jnp.concatenate([jnp.where(incl, gar[g][gc:, :gc], 0.0),
                                             jnp.where(incl, gar[g][gc:, gc:], 0.0)], axis=1).astype(BF16)
                            for g in groups]
            else:
                t["b_rep"], t["k_rep"] = b_rep, k_rep
                l_ab = [jnp.where(strict, nt(a_st[g], b_rep[g]), 0.0) for g in groups]
                t["a_ak"] = [jnp.where(strict, nt(a_st[g], k_rep[g]), 0.0).astype(BF16) for g in groups]
            t["lp"] = [x.astype(BF16) for x in l_ab]
            t["p"] = [eye + x for x in l_ab]

        def s_products_r():
            if not merged:
                t["rh"] = [rh_s[rs, ls] for ls in lanes]
                r_st = [stack(x) for x in t["rh"]]
                t["a_rb"] = [jnp.where(incl, nt(r_st[g], t["b_rep"][g]), 0.0).astype(BF16) for g in groups]
                t["a_rk"] = [jnp.where(incl, nt(r_st[g], t["k_rep"][g]), 0.0).astype(BF16) for g in groups]

        def s_av():
            t["v_st"] = [stack(v_s[rs, ls]) for ls in lanes]
            t["av"] = [_dot(t["a_ak"][g], t["v_st"][g]) for g in groups]

        def s_square():
            t["lp"] = [_dot(x, x).astype(BF16) for x in t["lp"]]

        def s_extend():
            t["p"] = [t["p"][g] + _dot(t["lp"][g], t["p"][g].astype(BF16)) for g in groups]

        def s_square_extend():
            both = [_dot(t["lp"][g], jnp.concatenate([t["lp"][g], t["p"][g].astype(BF16)], axis=1))
                    for g in groups]
            t["lp"] = [x[:, :gc].astype(BF16) for x in both]
            t["p"] = [t["p"][g] + both[g][:, gc:] for g in groups]

        def s_finish():
            t["pb"] = [x.astype(BF16) for x in t["p"]]
            t["b_st"] = [stack(x) for x in t["bt"]]
            t["k_st"] = [stack(x) for x in t["kt"]]

        stages = [s_products_a, s_products_r, s_av]
        factors = c.bit_length() - 1
        if merged:
            stages += [s_square] + [s_square_extend] * (factors - 2) + [s_extend]
        else:
            stages += [s_square, s_extend] * (factors - 1)
        return t, stages + [s_finish]

    def state_chain_stages(q, t, st):
        rs = t["rs"]
        w = {}

        def s_read_state():
            w["ars"] = [nt(jnp.concatenate([t["ah"][g], t["rh"][g]], axis=0), st[g].astype(BF16))
                        for g in groups]

        def s_solve():
            x_st = [stack(w["ars"][g][:c]) + t["av"][g] for g in groups]
            w["u"] = [_dot(t["pb"][g], x_st[g].astype(BF16)).astype(BF16) for g in groups]

        def s_outputs():
            if merged:
                uv = [jnp.concatenate([w["u"][g], t["v_st"][g]], axis=0) for g in groups]
                w["y"] = [_dot(t["a_r"][g], uv[g]) for g in groups]
                w["upd"] = [tn(uv[g], jnp.concatenate([t["b_st"][g], t["k_st"][g]], axis=0))
                            for g in groups]
            else:
                w["y"] = [_dot(t["a_rb"][g], w["u"][g]) + _dot(t["a_rk"][g], t["v_st"][g]) for g in groups]
                w["upd"] = [tn(w["u"][g], t["b_st"][g]) + tn(t["v_st"][g], t["k_st"][g]) for g in groups]

        def s_commit():
            for g in groups:
                y = w["ars"][g][c:]
                for hh in range(G):
                    y = y + w["y"][g][hh * c:(hh + 1) * c]
                y_s[rs, lanes[g]] = y
                st[g] = (st[g] + w["upd"][g]) * wl_s[q:q + 1, lanes[g]]

        return [s_read_state, s_solve, s_outputs, s_commit]

    st = [st_s[g] for g in groups]
    cur_t, first = state_free_stages(0)
    for stage in first:
        stage()
    for q in range(nchunk):
        chain = state_chain_stages(q, cur_t, st)
        nxt_t, free = state_free_stages(q + 1) if q + 1 < nchunk else (None, [])
        every = max(1, len(free) // len(chain))
        fi = 0
        for stage in chain:
            for _ in range(every):
                if fi < len(free):
                    free[fi]()
                    fi += 1
            stage()
        while fi < len(free):
            free[fi]()
            fi += 1
        cur_t = nxt_t
    for g in groups:
        st_s[g] = st[g]

    o_ref[0] = _mix_out(y_s[...], bonus, gate, sa_ref[0], mb_ref[0], x_ref[0],
                        lnw_ref, lnb_ref, wo_ref, e)

    @pl.when(j == pl.num_programs(1) - 1)
    def _():
        for h in range(H):
            d0 = (h % G) * HD
            so_ref[0, h] = st_s[h // G, d0:d0 + HD, d0:d0 + HD]


def _mix_seq(xs, mb, sa, x, s0, l, wts, e, *, tl, c):
    bsz, length, _ = x.shape
    assert length % tl == 0 and tl % c == 0
    nchunk = tl // c

    def tile(width):
        return pl.BlockSpec((1, tl, width), lambda b, j: (b, j, 0))

    return pl.pallas_call(
        functools.partial(_mix_seq_body, tl=tl, c=c),
        grid=(bsz, length // tl),
        in_specs=[tile(DS), tile(D), tile(D), tile(D), _const_spec((1, H, HD, HD))]
        + [_layer_spec(w, l) for w in wts] + [_const_spec(e.shape)],
        out_specs=[tile(D), pl.BlockSpec((1, H, HD, HD), lambda b, j: (b, 0, 0, 0))],
        out_shape=[
            jax.ShapeDtypeStruct((bsz, length, D), F32),
            jax.ShapeDtypeStruct((bsz, H, HD, HD), F32),
        ],
        scratch_shapes=[pltpu.VMEM((NG, GL, GL), F32)] + [pltpu.VMEM((tl, D), BF16)] * 5 + [
            pltpu.VMEM((max(8, nchunk), D), F32),
            pltpu.VMEM((tl, D), F32),
        ],
        compiler_params=pltpu.CompilerParams(
            dimension_semantics=("arbitrary", "arbitrary"), vmem_limit_bytes=VMEM_LIMIT),
        name=f"mix_seq_{tl}",
    )(xs, mb, sa, x, s0, *wts, e)


def _mix_step_body(xs_ref, mb_ref, sa_ref, x_ref, s_ref,
                   w0_ref, w2_ref, a0_ref, a2_ref, wg2_ref, kk_ref, ka_ref, rk_ref,
                   lnw_ref, lnb_ref, wo_ref, e_ref,
                   o_ref, so_ref,
                   a_s, wr_s, w_s, b_s, k_s, v_s, s1_s, s2_s, z_s, *, bt):
    e = e_ref[...]
    r, kh, v, kk, alr, dexp, gate, bonus = _mix_heads(
        lambda a, b: xs_ref[:, a:b],
        w0_ref, w2_ref, a0_ref, a2_ref, wg2_ref, kk_ref, ka_ref, rk_ref, e)
    w = jnp.exp(-dexp)
    bb = kk * alr
    lane = lax.broadcasted_iota(jnp.int32, (bt, LANES), 1)

    def fill(dst, val):
        for p in range(H // 2):
            blk = val[:, p * LANES:(p + 1) * LANES]
            rolled = pltpu.roll(blk, HD, 1)
            dst[2 * p] = jnp.where(lane < HD, blk, rolled)
            dst[2 * p + 1] = jnp.where(lane < HD, rolled, blk)

    fill(a_s, -kk)
    fill(wr_s, w * r)
    fill(w_s, w)
    fill(b_s, bb)
    fill(k_s, kh)
    fill(v_s, v)
    fill(s1_s, _seg_sum(bb * r, e))
    fill(s2_s, _seg_sum(kh * r, e))

    li = lax.broadcasted_iota(jnp.int32, (LANES, LANES), 0)
    lj = lax.broadcasted_iota(jnp.int32, (LANES, LANES), 1)
    half_sum = ((li // HD) == (lj // HD)).astype(BF16)
    half_sum2 = jnp.concatenate([half_sum, half_sum], axis=0)
    src_v = 2 * (li % HD) + li // HD
    perm_lo = ((li % HD < PK) & (lj == src_v)).astype(BF16)
    perm_hi = ((li % HD < PK) & (lj == src_v + HD)).astype(BF16)
    pi = lax.broadcasted_iota(jnp.int32, (PK, LANES), 0)
    pj = lax.broadcasted_iota(jnp.int32, (PK, LANES), 1)
    pick_v = ((pj % HD) == 2 * pi + pj // HD).astype(F32)
    diag = ((pj % HD) == pi).astype(F32)

    def seq_step(b, carry):
        rb = pl.ds(b, 1)
        xa, xw, xv = [], [], []
        for h in range(H):
            s = s_ref[b, h]
            xa.append((s * a_s[h, rb, :]).astype(BF16))
            xw.append((s * wr_s[h, rb, :]).astype(BF16))
            vm = pick_v * v_s[h, rb, :]
            vhi = vm.astype(BF16)
            xv.append(jnp.concatenate([vhi, (vm - vhi.astype(F32)).astype(BF16)], axis=1))
        sa = _dot(jnp.concatenate(xa, axis=0), half_sum)
        swr = _dot(jnp.concatenate(xw, axis=0), half_sum)
        vb = _dot(jnp.concatenate(xv, axis=0), half_sum2)
        for h in range(H):
            hs = slice(h * PK, (h + 1) * PK)
            so_ref[b, h] = (s_ref[b, h] * w_s[h, rb, :] + sa[hs] * b_s[h, rb, :]
                            + vb[hs] * k_s[h, rb, :])
            y = swr[hs] + sa[hs] * s1_s[h, rb, :] + vb[hs] * s2_s[h, rb, :]
            z_s[h, rb, :] = jnp.sum(y * diag, axis=0, keepdims=True)
        return carry

    lax.fori_loop(0, bt, seq_step, 0)
    y = jnp.concatenate(
        [_dot_split(z_s[2 * p], perm_lo) + _dot_split(z_s[2 * p + 1], perm_hi) for p in range(H // 2)],
        axis=1)
    o_ref[...] = _mix_out(y, bonus, gate, sa_ref[...], mb_ref[...], x_ref[...],
                          lnw_ref, lnb_ref, wo_ref, e)


def _mix_step(xs, mb, sa, x, states, l, wts, e):
    bsz = x.shape[0]
    bt = STEP_TILE
    assert bsz % bt == 0
    state_spec = pl.BlockSpec((None, bt, H, PK, LANES), lambda i: (l, i, 0, 0, 0))

    def rows(width):
        return pl.BlockSpec((bt, width), lambda i: (i, 0))

    tokens = (xs, mb, sa, x)
    return pl.pallas_call(
        functools.partial(_mix_step_body, bt=bt),
        grid=(bsz // bt,),
        in_specs=[rows(DS), rows(D), rows(D), rows(D), state_spec]
        + [_layer_spec(w, l) for w in wts] + [_const_spec(e.shape)],
        out_specs=[rows(D), state_spec],
        out_shape=[
            jax.ShapeDtypeStruct((bsz, D), F32),
            jax.ShapeDtypeStruct(states.shape, F32),
        ],
        scratch_shapes=[pltpu.VMEM((H, bt, LANES), F32)] * 9,
        input_output_aliases={len(tokens): 1},
        compiler_params=pltpu.CompilerParams(
            dimension_semantics=("arbitrary",), vmem_limit_bytes=VMEM_LIMIT),
        name="mix_step",
    )(*tokens, states, *wts, e)


def kernel(x_prompt, x_sample, state_wkv, state_shift, state_conv, meta_tokens, norm_ffn1, ffn1_w_in, ffn1_w_out, norm_mix, w_in, mu_shift, w0, w_w2, a0, w_a2, w_g2, k_k, k_a, r_k, lnx_w, lnx_b, conv_w, w_o, norm_ffn2, ffn2_w_in, ffn2_w_out, norm_final):
    depth = norm_ffn1.shape[0]
    bp, lp, _ = x_prompt.shape
    bs = x_sample.shape[0]
    n_meta = meta_tokens.shape[0]

    def vec(a):
        return a.reshape(depth, 1, -1)

    half_zeros = jnp.zeros((depth, LORA_WA // 2, D), F32)
    wts = tuple(dict(
        w0=vec(w0),
        w2=jnp.concatenate([w_w2, half_zeros], axis=1).astype(BF16),
        a0=vec(a0),
        a2=jnp.concatenate([half_zeros, w_a2], axis=1).astype(BF16),
        wg2=w_g2.astype(BF16), k_k=vec(k_k), k_a=vec(k_a), r_k=vec(r_k), lnw=vec(lnx_w),
        lnb=vec(lnx_b), wo=w_o.astype(BF16))[name] for name in _MIX_WEIGHTS)
    lane = jnp.arange(GL)
    e = (lane[:, None] // HD == lane[None, :] // HD).astype(BF16)
    n1, nm, n2, mu = vec(norm_ffn1), vec(norm_mix), vec(norm_ffn2), vec(mu_shift)
    f1_in, f1_out = ffn1_w_in.astype(BF16), ffn1_w_out.astype(BF16)
    f2_in, f2_out = ffn2_w_in.astype(BF16), ffn2_w_out.astype(BF16)
    w_in_b = w_in.astype(BF16)
    gfin = norm_final[None]

    xm = meta_tokens.astype(x_prompt.dtype)
    xp = x_prompt.reshape(bp * lp, D)
    xs = x_sample.reshape(bs, D)
    states = state_wkv.reshape(depth, bs, H, PK, LANES)
    zero_s = jnp.zeros((1, H, HD, HD), F32)
    zero_sh = jnp.zeros((1, 1, DS), F32)
    zero_cv = jnp.zeros((1, 2, D), F32)
    p_wkv, p_shift, p_conv, s_shift, s_conv = [], [], [], [], []
    def ffn_meta_sample(xm, xs, l, g, w1, w2):
        y = _ffn(jnp.concatenate([xm, xs], axis=0), l, g, w1, w2, gfin, False)
        return y[:n_meta], y[n_meta:]

    for l in range(depth):
        xm, xs = ffn_meta_sample(xm, xs, l, n1, f1_in, f1_out)
        xp = _ffn(xp, l, n1, f1_in, f1_out, gfin, False)
        m_xs, m_mb, m_sa, m_sh, m_cv = _proj_seq(xm, n_meta, l, nm, w_in_b, mu, conv_w, zero_sh, zero_cv)
        p_xs, p_mb, p_sa, p_sh, p_cv = _proj_seq(xp, lp, l, nm, w_in_b, mu, conv_w, m_sh, m_cv)
        s_xs, s_mb, s_sa, s_sh, s_cv = _proj_step(xs, l, nm, w_in_b, mu, conv_w, state_shift, state_conv)
        xm3, m_s = _mix_seq(m_xs.reshape(1, n_meta, DS), m_mb.reshape(1, n_meta, D),
                            m_sa.reshape(1, n_meta, D), xm.reshape(1, n_meta, D), zero_s,
                            l, wts, e, tl=n_meta, c=n_meta)
        xp3, p_s = _mix_seq(p_xs.reshape(bp, lp, DS), p_mb.reshape(bp, lp, D), p_sa.reshape(bp, lp, D),
                            xp.reshape(bp, lp, D), m_s, l, wts, e, tl=min(SEQ_TILE, lp), c=CHUNK)
        xs, states = _mix_step(s_xs, s_mb, s_sa, xs, states, l, wts, e)
        xm = xm3.reshape(n_meta, D)
        xp = xp3.reshape(bp * lp, D)
        last = l == depth - 1
        if last:
            xs = _ffn(xs, l, n2, f2_in, f2_out, gfin, True)
        else:
            xm, xs = ffn_meta_sample(xm, xs, l, n2, f2_in, f2_out)
        xp = _ffn(xp, l, n2, f2_in, f2_out, gfin, last)
        p_wkv.append(p_s)
        p_shift.append(p_sh[:, 0])
        p_conv.append(p_cv)
        s_shift.append(s_sh)
        s_conv.append(s_cv)
    return (xp.reshape(bp, lp, D), xs.reshape(bs, 1, D), jnp.stack(p_wkv), jnp.stack(p_shift),
            jnp.stack(p_conv), states.reshape(state_wkv.shape), jnp.stack(s_shift), jnp.stack(s_conv))
```

```python
import functools
import math

import jax
import jax.numpy as jnp
from jax import lax
from jax.experimental import pallas as pl
from jax.experimental.pallas import tpu as pltpu

F32 = jnp.float32
BF16 = jnp.bfloat16

D = 1024
H = 16
HD = 64
G = 4
GL = G * HD
NG = H // G
LORA_WA = 128
LORA_G = 128
DS = 3 * D + LORA_WA + LORA_G
DP = DS + 3 * D + 2 * D
DFF = 2816
RMS_EPS = 1e-6
GN_EPS = 64e-5
EXP_M05 = math.exp(-0.5)
LANES = 128
MXU_DIM = 256
PK = HD * HD // LANES

CHUNK = 32
SEQ_TILE = 512
ROW_TILE = 512
PROJ_TILE = 256
STEP_TILE = 32
FF_SPLIT = 2
VMEM_LIMIT = 56 * 1024 * 1024

_NT = (((1,), (1,)), ((), ()))
_TN = (((0,), (0,)), ((), ()))

_MIX_WEIGHTS = ("w0", "w2", "a0", "a2", "wg2", "k_k", "k_a", "r_k", "lnw", "lnb", "wo")


def _dot(a, b):
    return jnp.dot(a, b, preferred_element_type=F32)


def _dot_split(x, m):
    hi = x.astype(BF16)
    lo = (x - hi.astype(F32)).astype(BF16)
    return _dot(hi, m) + _dot(lo, m)


def _rmsnorm(x, g):
    return x * lax.rsqrt(jnp.mean(x * x, axis=-1, keepdims=True) + RMS_EPS) * g


def _sigmoid(x):
    return 1.0 / (1.0 + jnp.exp(-x))


def _const_spec(shape):
    nd = len(shape)
    return pl.BlockSpec(shape, lambda *_: (0,) * nd)


def _layer_spec(arr, l, **kw):
    nd = arr.ndim - 1
    return pl.BlockSpec((None,) + arr.shape[1:], lambda *_: (l,) + (0,) * nd, **kw)


def _ffn_body(x_ref, g_ref, win_ref, wout_ref, gf_ref, o_ref, *, final_norm):
    x = x_ref[...]
    xb = _rmsnorm(x, g_ref[...]).astype(BF16)
    piece = DFF // FF_SPLIT
    acc = None
    for c in range(FF_SPLIT):
        gate = _dot(xb, win_ref[:, c * piece:(c + 1) * piece])
        up = _dot(xb, win_ref[:, DFF + c * piece:DFF + (c + 1) * piece])
        act = (gate * _sigmoid(gate) * up).astype(BF16)
        part = _dot(act, wout_ref[c * piece:(c + 1) * piece, :])
        acc = part if acc is None else acc + part
    y = x + 0.5 * acc
    if final_norm:
        y = _rmsnorm(y, gf_ref[...])
    o_ref[...] = y


def _ffn(x, l, g, w_in, w_out, g_final, final_norm):
    rows = x.shape[0]
    tm = min(ROW_TILE, rows)
    assert rows % tm == 0
    return pl.pallas_call(
        functools.partial(_ffn_body, final_norm=final_norm),
        grid=(rows // tm,),
        in_specs=[
            pl.BlockSpec((tm, D), lambda i: (i, 0)),
            _layer_spec(g, l),
            _layer_spec(w_in, l, pipeline_mode=pl.Buffered(1)),
            _layer_spec(w_out, l, pipeline_mode=pl.Buffered(1)),
            _const_spec((1, D)),
        ],
        out_specs=pl.BlockSpec((tm, D), lambda i: (i, 0)),
        out_shape=jax.ShapeDtypeStruct((rows, D), F32),
        compiler_params=pltpu.CompilerParams(
            dimension_semantics=("arbitrary",), vmem_limit_bytes=VMEM_LIMIT),
        name="ffn",
    )(x, g, w_in, w_out, g_final)


def _proj_columns(x_ref, g_ref, w_ref):
    xb = _rmsnorm(x_ref[...], g_ref[...]).astype(BF16)
    ps = _dot(xb, w_ref[:, 0:DS])
    pc = _dot(xb, w_ref[:, DS:DS + 3 * D])
    pg = _dot(xb, w_ref[:, DS + 3 * D:DP])
    return ps, pc, pg


def _proj_seq_body(x_ref, g_ref, w_ref, mu_ref, cw_ref, sh0_ref, cv0_ref,
                   xs_ref, mb_ref, sa_ref, sh_ref, cv_ref, shc_s, cvc_s, *, tm, tps):
    i = pl.program_id(0)

    @pl.when(i % tps == 0)
    def _():
        shc_s[...] = sh0_ref[0]
        cvc_s[...] = cv0_ref[0]

    xb = _rmsnorm(x_ref[...], g_ref[...]).astype(BF16)
    row = lax.broadcasted_iota(jnp.int32, (tm, 1), 0)

    pc = _dot(xb, w_ref[:, DS:DS + 3 * D])
    u = pc[:, D:2 * D] * pc[:, 2 * D:3 * D]
    c0 = cvc_s[0:1, :]
    c1 = cvc_s[1:2, :]
    u1 = jnp.where(row == 0, c1, pltpu.roll(u, 1, 0))
    u2 = jnp.where(row == 0, c0, jnp.where(row == 1, c1, pltpu.roll(u, 2, 0)))
    z = cw_ref[0:1, :] * u2 + cw_ref[1:2, :] * u1 + cw_ref[2:3, :] * u
    pg = _dot(xb, w_ref[:, DS + 3 * D:DP])
    mb_ref[...] = _sigmoid(pg[:, D:2 * D]) * (pc[:, 0:D] * z)
    sa_ref[...] = _sigmoid(pg[:, 0:D])
    last_u = u[tm - 2:tm, :]
    cvc_s[...] = last_u

    ps = _dot(xb, w_ref[:, 0:DS])
    prev = jnp.where(row == 0, shc_s[...], pltpu.roll(ps, 1, 0))
    xs_ref[...] = ps + (prev - ps) * mu_ref[...]
    last_ps = ps[tm - 1:tm, :]
    shc_s[...] = last_ps

    @pl.when(i % tps == tps - 1)
    def _():
        sh_ref[0] = last_ps
        cv_ref[0] = last_u


def _proj_seq(x, seq_len, l, g, w, mu, cw, sh0, cv0):
    rows = x.shape[0]
    tm = min(PROJ_TILE, seq_len)
    assert rows % seq_len == 0 and seq_len % tm == 0 and tm >= 2
    nseq, tps = rows // seq_len, seq_len // tm
    return pl.pallas_call(
        functools.partial(_proj_seq_body, tm=tm, tps=tps),
        grid=(rows // tm,),
        in_specs=[
            pl.BlockSpec((tm, D), lambda i: (i, 0)),
            _layer_spec(g, l),
            _layer_spec(w, l, pipeline_mode=pl.Buffered(1)),
            _layer_spec(mu, l),
            _layer_spec(cw, l),
            _const_spec((1, 1, DS)),
            _const_spec((1, 2, D)),
        ],
        out_specs=[
            pl.BlockSpec((tm, DS), lambda i: (i, 0)),
            pl.BlockSpec((tm, D), lambda i: (i, 0)),
            pl.BlockSpec((tm, D), lambda i: (i, 0)),
            pl.BlockSpec((1, 1, DS), lambda i: (i // tps, 0, 0)),
            pl.BlockSpec((1, 2, D), lambda i: (i // tps, 0, 0)),
        ],
        out_shape=[
            jax.ShapeDtypeStruct((rows, DS), F32),
            jax.ShapeDtypeStruct((rows, D), F32),
            jax.ShapeDtypeStruct((rows, D), F32),
            jax.ShapeDtypeStruct((nseq, 1, DS), F32),
            jax.ShapeDtypeStruct((nseq, 2, D), F32),
        ],
        scratch_shapes=[pltpu.VMEM((1, DS), F32), pltpu.VMEM((2, D), F32)],
        compiler_params=pltpu.CompilerParams(
            dimension_semantics=("arbitrary",), vmem_limit_bytes=VMEM_LIMIT),
        name="proj_seq",
    )(x, g, w, mu, cw, sh0, cv0)


def _proj_step_body(x_ref, g_ref, w_ref, mu_ref, cw_ref, sh_ref, cv_ref,
                    xs_ref, mb_ref, sa_ref, sho_ref, cvo_ref):
    ps, pc, pg = _proj_columns(x_ref, g_ref, w_ref)
    xs_ref[...] = ps + (sh_ref[...] - ps) * mu_ref[...]
    sho_ref[...] = ps
    u = pc[:, D:2 * D] * pc[:, 2 * D:3 * D]
    c1 = cv_ref[:, 1, :]
    z = cw_ref[0:1, :] * cv_ref[:, 0, :] + cw_ref[1:2, :] * c1 + cw_ref[2:3, :] * u
    mb_ref[...] = _sigmoid(pg[:, D:2 * D]) * (pc[:, 0:D] * z)
    sa_ref[...] = _sigmoid(pg[:, 0:D])
    cvo_ref[:, 0, :] = c1
    cvo_ref[:, 1, :] = u


def _proj_step(x, l, g, w, mu, cw, sh_all, cv_all):
    rows = x.shape[0]
    return pl.pallas_call(
        _proj_step_body,
        grid=(1,),
        in_specs=[
            _const_spec((rows, D)),
            _layer_spec(g, l),
            _layer_spec(w, l, pipeline_mode=pl.Buffered(1)),
            _layer_spec(mu, l),
            _layer_spec(cw, l),
            _layer_spec(sh_all, l),
            _layer_spec(cv_all, l),
        ],
        out_specs=[
            _const_spec((rows, DS)),
            _const_spec((rows, D)),
            _const_spec((rows, D)),
            _const_spec((rows, DS)),
            _const_spec((rows, 2, D)),
        ],
        out_shape=[
            jax.ShapeDtypeStruct((rows, DS), F32),
            jax.ShapeDtypeStruct((rows, D), F32),
            jax.ShapeDtypeStruct((rows, D), F32),
            jax.ShapeDtypeStruct((rows, DS), F32),
            jax.ShapeDtypeStruct((rows, 2, D), F32),
        ],
        compiler_params=pltpu.CompilerParams(
            dimension_semantics=("arbitrary",), vmem_limit_bytes=VMEM_LIMIT),
        name="proj_step",
    )(x, g, w, mu, cw, sh_all, cv_all)


def _seg_sum(x, e):
    return jnp.concatenate([_dot(x[:, g * GL:(g + 1) * GL].astype(BF16), e) for g in range(NG)], axis=1)


def _mix_heads(xs, w0_ref, w2_ref, a0_ref, a2_ref, wg2_ref, kk_ref, ka_ref, rk_ref, e):
    r = xs(0, D)
    k = xs(D, 2 * D)
    v = xs(2 * D, 3 * D)
    lwla = xs(3 * D, 3 * D + LORA_WA)
    lg = xs(3 * D + LORA_WA, DS)
    wl = w0_ref[...] + _dot(jnp.tanh(lwla).astype(BF16), w2_ref[...])
    dexp = EXP_M05 * _sigmoid(wl)
    alr = _sigmoid(a0_ref[...] + _dot(lwla.astype(BF16), a2_ref[...]))
    gate = _dot(_sigmoid(lg).astype(BF16), wg2_ref[...])
    kkr = k * kk_ref[...]
    kk = kkr * lax.rsqrt(jnp.maximum(_seg_sum(kkr * kkr, e), 1e-24))
    kh = k * (1.0 + (alr - 1.0) * ka_ref[...])
    bonus = _seg_sum(r * kh * rk_ref[...], e) * v
    return r, kh, v, kk, alr, dexp, gate, bonus


def _mix_out(y, bonus, gate, sa, mb, x, lnw_ref, lnb_ref, wo_ref, e):
    mu = _seg_sum(y, e) * (1.0 / HD)
    d = y - mu
    var = _seg_sum(d * d, e) * (1.0 / HD)
    yn = d * lax.rsqrt(var + GN_EPS) * lnw_ref[...] + lnb_ref[...]
    m = sa * ((yn + bonus) * gate) + mb
    return x + _dot(m.astype(BF16), wo_ref[...])


def _mix_seq_body(xs_ref, mb_ref, sa_ref, x_ref, s0_ref,
                  w0_ref, w2_ref, a0_ref, a2_ref, wg2_ref, kk_ref, ka_ref, rk_ref,
                  lnw_ref, lnb_ref, wo_ref, e_ref,
                  o_ref, so_ref,
                  st_s, ah_s, rh_s, bt_s, kt_s, v_s, wl_s, y_s, *, tl, c):
    j = pl.program_id(1)
    gc = G * c
    nchunk = tl // c

    @pl.when(j == 0)
    def _():
        st_s[...] = jnp.zeros_like(st_s)
        for h in range(H):
            d0 = (h % G) * HD
            st_s[h // G, d0:d0 + HD, d0:d0 + HD] = s0_ref[0, h]

    e = e_ref[...]
    row = lax.broadcasted_iota(jnp.int32, (tl, 1), 0)
    r, kh, v, kk, alr, dexp, gate, bonus = _mix_heads(
        lambda a, b: xs_ref[0, :, a:b],
        w0_ref, w2_ref, a0_ref, a2_ref, wg2_ref, kk_ref, ka_ref, rk_ref, e)

    rin = row % c
    cum = dexp
    s = 1
    while s < c:
        cum = cum + jnp.where(rin >= s, pltpu.roll(cum, s, 0), 0.0)
        s *= 2
    wc = jnp.exp(-cum)
    winv = jnp.exp(cum)
    rh_s[...] = (r * wc).astype(BF16)
    ah_s[...] = (-kk * jnp.exp(dexp - cum)).astype(BF16)
    bt_s[...] = (kk * alr * winv).astype(BF16)
    kt_s[...] = (kh * winv).astype(BF16)
    v_s[...] = v.astype(BF16)
    for ci in range(nchunk):
        wl_s[ci:ci + 1, :] = wc[(ci + 1) * c - 1:(ci + 1) * c, :]

    groups = range(NG)
    lanes = [slice(g * GL, (g + 1) * GL) for g in groups]

    ri = lax.broadcasted_iota(jnp.int32, (gc, gc), 0)
    cj = lax.broadcasted_iota(jnp.int32, (gc, gc), 1)
    same = (ri // c) == (cj // c)
    strict = same & ((ri % c) > (cj % c))
    incl = same & ((ri % c) >= (cj % c))
    eye = (ri == cj).astype(F32)
    smask = (lax.broadcasted_iota(jnp.int32, (gc, GL), 0) // c) == (
        lax.broadcasted_iota(jnp.int32, (gc, GL), 1) // HD)

    def stack(x):
        return jnp.where(smask, jnp.concatenate([x] * G, axis=0), jnp.zeros((), x.dtype))

    def nt(a, b):
        return lax.dot_general(a, b, _NT, preferred_element_type=F32)

    def tn(a, b):
        return lax.dot_general(a, b, _TN, preferred_element_type=F32)

    merged = gc % LANES == 0 and 2 * gc <= MXU_DIM

    def state_free_stages(q):
        rs = slice(q * c, (q + 1) * c)
        t = {"rs": rs}

        def s_products_a():
            t["ah"] = [ah_s[rs, ls] for ls in lanes]
            t["bt"] = [bt_s[rs, ls] for ls in lanes]
            t["kt"] = [kt_s[rs, ls] for ls in lanes]
            a_st = [stack(x) for x in t["ah"]]
            b_rep = [jnp.concatenate([x] * G, axis=0) for x in t["bt"]]
            k_rep = [jnp.concatenate([x] * G, axis=0) for x in t["kt"]]
            if merged:
                t["rh"] = [rh_s[rs, ls] for ls in lanes]
                ar_st = [jnp.concatenate([a_st[g], stack(t["rh"][g])], axis=0) for g in groups]
                bk = [jnp.concatenate([b_rep[g], k_rep[g]], axis=0) for g in groups]
                gar = [nt(ar_st[g], bk[g]) for g in groups]
                l_ab = [jnp.where(strict, gar[g][:gc, :gc], 0.0) for g in groups]
                t["a_ak"] = [jnp.where(strict, gar[g][:gc, gc:], 0.0).astype(BF16) for g in groups]
                t["a_r"] = [jnp.concatenate([jnp.where(incl, gar[g][gc:, :gc], 0.0),
                                             jnp.where(incl, gar[g][gc:, gc:], 0.0)], axis=1).astype(BF16)
                            for g in groups]
            else:
                t["b_rep"], t["k_rep"] = b_rep, k_rep
                l_ab = [jnp.where(strict, nt(a_st[g], b_rep[g]), 0.0) for g in groups]
                t["a_ak"] = [jnp.where(strict, nt(a_st[g], k_rep[g]), 0.0).astype(BF16) for g in groups]
            t["lp"] = [x.astype(BF16) for x in l_ab]
            t["p"] = [eye + x for x in l_ab]

        def s_products_r():
            if not merged:
                t["rh"] = [rh_s[rs, ls] for ls in lanes]
                r_st = [stack(x) for x in t["rh"]]
                t["a_rb"] = [jnp.where(incl, nt(r_st[g], t["b_rep"][g]), 0.0).astype(BF16) for g in groups]
                t["a_rk"] = [jnp.where(incl, nt(r_st[g], t["k_rep"][g]), 0.0).astype(BF16) for g in groups]

        def s_av():
            t["v_st"] = [stack(v_s[rs, ls]) for ls in lanes]
            t["av"] = [_dot(t["a_ak"][g], t["v_st"][g]) for g in groups]

        def s_square():
            t["lp"] = [_dot(x, x).astype(BF16) for x in t["lp"]]

        def s_extend():
            t["p"] = [t["p"][g] + _dot(t["lp"][g], t["p"][g].astype(BF16)) for g in groups]

        def s_square_extend():
            both = [_dot(t["lp"][g], jnp.concatenate([t["lp"][g], t["p"][g].astype(BF16)], axis=1))
                    for g in groups]
            t["lp"] = [x[:, :gc].astype(BF16) for x in both]
            t["p"] = [t["p"][g] + both[g][:, gc:] for g in groups]

        def s_finish():
            t["pb"] = [x.astype(BF16) for x in t["p"]]
            t["b_st"] = [stack(x) for x in t["bt"]]
            t["k_st"] = [stack(x) for x in t["kt"]]

        stages = [s_products_a, s_products_r, s_av]
        factors = c.bit_length() - 1
        if merged:
            stages += [s_square] + [s_square_extend] * (factors - 2) + [s_extend]
        else:
            stages += [s_square, s_extend] * (factors - 1)
        return t, stages + [s_finish]

    def state_chain_stages(q, t, st):
        rs = t["rs"]
        w = {}

        def s_read_state():
            w["ars"] = [nt(jnp.concatenate([t["ah"][g], t["rh"][g]], axis=0), st[g].astype(BF16))
                        for g in groups]

        def s_solve():
            x_st = [stack(w["ars"][g][:c]) + t["av"][g] for g in groups]
            w["u"] = [_dot(t["pb"][g], x_st[g].astype(BF16)).astype(BF16) for g in groups]

        def s_outputs():
            if merged:
                uv = [jnp.concatenate([w["u"][g], t["v_st"][g]], axis=0) for g in groups]
                w["y"] = [_dot(t["a_r"][g], uv[g]) for g in groups]
                w["upd"] = [tn(uv[g], jnp.concatenate([t["b_st"][g], t["k_st"][g]], axis=0))
                            for g in groups]
            else:
                w["y"] = [_dot(t["a_rb"][g], w["u"][g]) + _dot(t["a_rk"][g], t["v_st"][g]) for g in groups]
                w["upd"] = [tn(w["u"][g], t["b_st"][g]) + tn(t["v_st"][g], t["k_st"][g]) for g in groups]

        def s_commit():
            for g in groups:
                y = w["ars"][g][c:]
                for hh in range(G):
                    y = y + w["y"][g][hh * c:(hh + 1) * c]
                y_s[rs, lanes[g]] = y
                st[g] = (st[g] + w["upd"][g]) * wl_s[q:q + 1, lanes[g]]

        return [s_read_state, s_solve, s_outputs, s_commit]

    st = [st_s[g] for g in groups]
    cur_t, first = state_free_stages(0)
    for stage in first:
        stage()
    for q in range(nchunk):
        chain = state_chain_stages(q, cur_t, st)
        nxt_t, free = state_free_stages(q + 1) if q + 1 < nchunk else (None, [])
        every = max(1, len(free) // len(chain))
        fi = 0
        for stage in chain:
            for _ in range(every):
                if fi < len(free):
                    free[fi]()
                    fi += 1
            stage()
        while fi < len(free):
            free[fi]()
            fi += 1
        cur_t = nxt_t
    for g in groups:
        st_s[g] = st[g]

    o_ref[0] = _mix_out(y_s[...], bonus, gate, sa_ref[0], mb_ref[0], x_ref[0],
                        lnw_ref, lnb_ref, wo_ref, e)

    @pl.when(j == pl.num_programs(1) - 1)
    def _():
        for h in range(H):
            d0 = (h % G) * HD
            so_ref[0, h] = st_s[h // G, d0:d0 + HD, d0:d0 + HD]


def _mix_seq(xs, mb, sa, x, s0, l, wts, e, *, tl, c):
    bsz, length, _ = x.shape
    assert length % tl == 0 and tl % c == 0
    nchunk = tl // c

    def tile(width):
        return pl.BlockSpec((1, tl, width), lambda b, j: (b, j, 0))

    return pl.pallas_call(
        functools.partial(_mix_seq_body, tl=tl, c=c),
        grid=(bsz, length // tl),
        in_specs=[tile(DS), tile(D), tile(D), tile(D), _const_spec((1, H, HD, HD))]
        + [_layer_spec(w, l) for w in wts] + [_const_spec(e.shape)],
        out_specs=[tile(D), pl.BlockSpec((1, H, HD, HD), lambda b, j: (b, 0, 0, 0))],
        out_shape=[
            jax.ShapeDtypeStruct((bsz, length, D), F32),
            jax.ShapeDtypeStruct((bsz, H, HD, HD), F32),
        ],
        scratch_shapes=[pltpu.VMEM((NG, GL, GL), F32)] + [pltpu.VMEM((tl, D), BF16)] * 5 + [
            pltpu.VMEM((max(8, nchunk), D), F32),
            pltpu.VMEM((tl, D), F32),
        ],
        compiler_params=pltpu.CompilerParams(
            dimension_semantics=("arbitrary", "arbitrary"), vmem_limit_bytes=VMEM_LIMIT),
        name=f"mix_seq_{tl}",
    )(xs, mb, sa, x, s0, *wts, e)


def _mix_step_body(xs_ref, mb_ref, sa_ref, x_ref, s_ref,
                   w0_ref, w2_ref, a0_ref, a2_ref, wg2_ref, kk_ref, ka_ref, rk_ref,
                   lnw_ref, lnb_ref, wo_ref, e_ref,
                   o_ref, so_ref,
                   a_s, wr_s, w_s, b_s, k_s, v_s, s1_s, s2_s, z_s, *, bt):
    e = e_ref[...]
    r, kh, v, kk, alr, dexp, gate, bonus = _mix_heads(
        lambda a, b: xs_ref[:, a:b],
        w0_ref, w2_ref, a0_ref, a2_ref, wg2_ref, kk_ref, ka_ref, rk_ref, e)
    w = jnp.exp(-dexp)
    bb = kk * alr
    lane = lax.broadcasted_iota(jnp.int32, (bt, LANES), 1)

    def fill(dst, val):
        for p in range(H // 2):
            blk = val[:, p * LANES:(p + 1) * LANES]
            rolled = pltpu.roll(blk, HD, 1)
            dst[2 * p] = jnp.where(lane < HD, blk, rolled)
            dst[2 * p + 1] = jnp.where(lane < HD, rolled, blk)

    fill(a_s, -kk)
    fill(wr_s, w * r)
    fill(w_s, w)
    fill(b_s, bb)
    fill(k_s, kh)
    fill(v_s, v)
    fill(s1_s, _seg_sum(bb * r, e))
    fill(s2_s, _seg_sum(kh * r, e))

    li = lax.broadcasted_iota(jnp.int32, (LANES, LANES), 0)
    lj = lax.broadcasted_iota(jnp.int32, (LANES, LANES), 1)
    half_sum = ((li // HD) == (lj // HD)).astype(BF16)
    half_sum2 = jnp.concatenate([half_sum, half_sum], axis=0)
    src_v = 2 * (li % HD) + li // HD
    perm_lo = ((li % HD < PK) & (lj == src_v)).astype(BF16)
    perm_hi = ((li % HD < PK) & (lj == src_v + HD)).astype(BF16)
    pi = lax.broadcasted_iota(jnp.int32, (PK, LANES), 0)
    pj = lax.broadcasted_iota(jnp.int32, (PK, LANES), 1)
    pick_v = ((pj % HD) == 2 * pi + pj // HD).astype(F32)
    diag = ((pj % HD) == pi).astype(F32)

    def seq_step(b, carry):
        rb = pl.ds(b, 1)
        xa, xw, xv = [], [], []
        for h in range(H):
            s = s_ref[b, h]
            xa.append((s * a_s[h, rb, :]).astype(BF16))
            xw.append((s * wr_s[h, rb, :]).astype(BF16))
            vm = pick_v * v_s[h, rb, :]
            vhi = vm.astype(BF16)
            xv.append(jnp.concatenate([vhi, (vm - vhi.astype(F32)).astype(BF16)], axis=1))
        sa = _dot(jnp.concatenate(xa, axis=0), half_sum)
        swr = _dot(jnp.concatenate(xw, axis=0), half_sum)
        vb = _dot(jnp.concatenate(xv, axis=0), half_sum2)
        for h in range(H):
            hs = slice(h * PK, (h + 1) * PK)
            so_ref[b, h] = (s_ref[b, h] * w_s[h, rb, :] + sa[hs] * b_s[h, rb, :]
                            + vb[hs] * k_s[h, rb, :])
            y = swr[hs] + sa[hs] * s1_s[h, rb, :] + vb[hs] * s2_s[h, rb, :]
            z_s[h, rb, :] = jnp.sum(y * diag, axis=0, keepdims=True)
        return carry

    lax.fori_loop(0, bt, seq_step, 0)
    y = jnp.concatenate(
        [_dot_split(z_s[2 * p], perm_lo) + _dot_split(z_s[2 * p + 1], perm_hi) for p in range(H // 2)],
        axis=1)
    o_ref[...] = _mix_out(y, bonus, gate, sa_ref[...], mb_ref[...], x_ref[...],
                          lnw_ref, lnb_ref, wo_ref, e)


def _mix_step(xs, mb, sa, x, states, l, wts, e):
    bsz = x.shape[0]
    bt = STEP_TILE
    assert bsz % bt == 0
    state_spec = pl.BlockSpec((None, bt, H, PK, LANES), lambda i: (l, i, 0, 0, 0))

    def rows(width):
        return pl.BlockSpec((bt, width), lambda i: (i, 0))

    tokens = (xs, mb, sa, x)
    return pl.pallas_call(
        functools.partial(_mix_step_body, bt=bt),
        grid=(bsz // bt,),
        in_specs=[rows(DS), rows(D), rows(D), rows(D), state_spec]
        + [_layer_spec(w, l) for w in wts] + [_const_spec(e.shape)],
        out_specs=[rows(D), state_spec],
        out_shape=[
            jax.ShapeDtypeStruct((bsz, D), F32),
            jax.ShapeDtypeStruct(states.shape, F32),
        ],
        scratch_shapes=[pltpu.VMEM((H, bt, LANES), F32)] * 9,
        input_output_aliases={len(tokens): 1},
        compiler_params=pltpu.CompilerParams(
            dimension_semantics=("arbitrary",), vmem_limit_bytes=VMEM_LIMIT),
        name="mix_step",
    )(*tokens, states, *wts, e)


def kernel(x_prompt, x_sample, state_wkv, state_shift, state_conv, meta_tokens, norm_ffn1, ffn1_w_in, ffn1_w_out, norm_mix, w_in, mu_shift, w0, w_w2, a0, w_a2, w_g2, k_k, k_a, r_k, lnx_w, lnx_b, conv_w, w_o, norm_ffn2, ffn2_w_in, ffn2_w_out, norm_final):
    depth = norm_ffn1.shape[0]
    bp, lp, _ = x_prompt.shape
    bs = x_sample.shape[0]
    n_meta = meta_tokens.shape[0]

    def vec(a):
        return a.reshape(depth, 1, -1)

    half_zeros = jnp.zeros((depth, LORA_WA // 2, D), F32)
    wts = tuple(dict(
        w0=vec(w0),
        w2=jnp.concatenate([w_w2, half_zeros], axis=1).astype(BF16),
        a0=vec(a0),
        a2=jnp.concatenate([half_zeros, w_a2], axis=1).astype(BF16),
        wg2=w_g2.astype(BF16), k_k=vec(k_k), k_a=vec(k_a), r_k=vec(r_k), lnw=vec(lnx_w),
        lnb=vec(lnx_b), wo=w_o.astype(BF16))[name] for name in _MIX_WEIGHTS)
    lane = jnp.arange(GL)
    e = (lane[:, None] // HD == lane[None, :] // HD).astype(BF16)
    n1, nm, n2, mu = vec(norm_ffn1), vec(norm_mix), vec(norm_ffn2), vec(mu_shift)
    f1_in, f1_out = ffn1_w_in.astype(BF16), ffn1_w_out.astype(BF16)
    f2_in, f2_out = ffn2_w_in.astype(BF16), ffn2_w_out.astype(BF16)
    w_in_b = w_in.astype(BF16)
    gfin = norm_final[None]

    xm = meta_tokens.astype(x_prompt.dtype)
    xp = x_prompt.reshape(bp * lp, D)
    xs = x_sample.reshape(bs, D)
    states = state_wkv.reshape(depth, bs, H, PK, LANES)
    zero_s = jnp.zeros((1, H, HD, HD), F32)
    zero_sh = jnp.zeros((1, 1, DS), F32)
    zero_cv = jnp.zeros((1, 2, D), F32)
    p_wkv, p_shift, p_conv, s_shift, s_conv = [], [], [], [], []
    def ffn_meta_sample(xm, xs, l, g, w1, w2):
        y = _ffn(jnp.concatenate([xm, xs], axis=0), l, g, w1, w2, gfin, False)
        return y[:n_meta], y[n_meta:]

    for l in range(depth):
        xm, xs = ffn_meta_sample(xm, xs, l, n1, f1_in, f1_out)
        xp = _ffn(xp, l, n1, f1_in, f1_out, gfin, False)
        m_xs, m_mb, m_sa, m_sh, m_cv = _proj_seq(xm, n_meta, l, nm, w_in_b, mu, conv_w, zero_sh, zero_cv)
        p_xs, p_mb, p_sa, p_sh, p_cv = _proj_seq(xp, lp, l, nm, w_in_b, mu, conv_w, m_sh, m_cv)
        s_xs, s_mb, s_sa, s_sh, s_cv = _proj_step(xs, l, nm, w_in_b, mu, conv_w, state_shift, state_conv)
        xm3, m_s = _mix_seq(m_xs.reshape(1, n_meta, DS), m_mb.reshape(1, n_meta, D),
                            m_sa.reshape(1, n_meta, D), xm.reshape(1, n_meta, D), zero_s,
                            l, wts, e, tl=n_meta, c=n_meta)
        xp3, p_s = _mix_seq(p_xs.reshape(bp, lp, DS), p_mb.reshape(bp, lp, D), p_sa.reshape(bp, lp, D),
                            xp.reshape(bp, lp, D), m_s, l, wts, e, tl=min(SEQ_TILE, lp), c=CHUNK)
        xs, states = _mix_step(s_xs, s_mb, s_sa, xs, states, l, wts, e)
        xm = xm3.reshape(n_meta, D)
        xp = xp3.reshape(bp * lp, D)
        last = l == depth - 1
        if last:
            xs = _ffn(xs, l, n2, f2_in, f2_out, gfin, True)
        else:
            xm, xs = ffn_meta_sample(xm, xs, l, n2, f2_in, f2_out)
        xp = _ffn(xp, l, n2, f2_in, f2_out, gfin, last)
        p_wkv.append(p_s)
        p_shift.append(p_sh[:, 0])
        p_conv.append(p_cv)
        s_shift.append(s_sh)
        s_conv.append(s_cv)
    return (xp.reshape(bp, lp, D), xs.reshape(bs, 1, D), jnp.stack(p_wkv), jnp.stack(p_shift),
            jnp.stack(p_conv), states.reshape(state_wkv.shape), jnp.stack(s_shift), jnp.stack(s_conv))
```

```python
import functools
import math

import jax
import jax.numpy as jnp
from jax import lax
from jax.experimental import pallas as pl
from jax.experimental.pallas import tpu as pltpu

F32 = jnp.float32
BF16 = jnp.bfloat16

D = 1024
H = 16
HD = 64
G = 4
GL = G * HD
NG = H // G
LORA_WA = 128
LORA_G = 128
DS = 3 * D + LORA_WA + LORA_G
DP = DS + 3 * D + 2 * D
DFF = 2816
RMS_EPS = 1e-6
GN_EPS = 64e-5
EXP_M05 = math.exp(-0.5)
LANES = 128
MXU_DIM = 256
PK = HD * HD // LANES

CHUNK = 32
SEQ_TILE = 512
ROW_TILE = 512
PROJ_TILE = 256
STEP_TILE = 32
VMEM_LIMIT = 56 * 1024 * 1024

_NT = (((1,), (1,)), ((), ()))
_TN = (((0,), (0,)), ((), ()))

_MIX_WEIGHTS = ("w0", "w2", "a0", "a2", "wg2", "k_k", "k_a", "r_k", "lnw", "lnb", "wo")


def _dot(a, b):
    return jnp.dot(a, b, preferred_element_type=F32)


def _dot_split(x, m):
    hi = x.astype(BF16)
    lo = (x - hi.astype(F32)).astype(BF16)
    return _dot(hi, m) + _dot(lo, m)


def _rmsnorm(x, g):
    return x * lax.rsqrt(jnp.mean(x * x, axis=-1, keepdims=True) + RMS_EPS) * g


def _sigmoid(x):
    return 1.0 / (1.0 + jnp.exp(-x))


def _const_spec(shape):
    nd = len(shape)
    return pl.BlockSpec(shape, lambda *_: (0,) * nd)


def _layer_spec(arr, l, **kw):
    nd = arr.ndim - 1
    return pl.BlockSpec((None,) + arr.shape[1:], lambda *_: (l,) + (0,) * nd, **kw)


def _ffn_body(x_ref, g_ref, win_ref, wout_ref, gf_ref, o_ref, *, final_norm):
    x = x_ref[...]
    xb = _rmsnorm(x, g_ref[...]).astype(BF16)
    gate = _dot(xb, win_ref[:, 0:DFF])
    up = _dot(xb, win_ref[:, DFF:2 * DFF])
    act = (gate * _sigmoid(gate) * up).astype(BF16)
    y = x + 0.5 * _dot(act, wout_ref[...])
    if final_norm:
        y = _rmsnorm(y, gf_ref[...])
    o_ref[...] = y


def _ffn(x, l, g, w_in, w_out, g_final, final_norm):
    rows = x.shape[0]
    tm = min(ROW_TILE, rows)
    assert rows % tm == 0
    return pl.pallas_call(
        functools.partial(_ffn_body, final_norm=final_norm),
        grid=(rows // tm,),
        in_specs=[
            pl.BlockSpec((tm, D), lambda i: (i, 0)),
            _layer_spec(g, l),
            _layer_spec(w_in, l, pipeline_mode=pl.Buffered(1)),
            _layer_spec(w_out, l, pipeline_mode=pl.Buffered(1)),
            _const_spec((1, D)),
        ],
        out_specs=pl.BlockSpec((tm, D), lambda i: (i, 0)),
        out_shape=jax.ShapeDtypeStruct((rows, D), F32),
        compiler_params=pltpu.CompilerParams(
            dimension_semantics=("arbitrary",), vmem_limit_bytes=VMEM_LIMIT),
        name="ffn",
    )(x, g, w_in, w_out, g_final)


def _proj_columns(x_ref, g_ref, w_ref):
    xb = _rmsnorm(x_ref[...], g_ref[...]).astype(BF16)
    ps = _dot(xb, w_ref[:, 0:DS])
    pc = _dot(xb, w_ref[:, DS:DS + 3 * D])
    pg = _dot(xb, w_ref[:, DS + 3 * D:DP])
    return ps, pc, pg


def _proj_seq_body(x_ref, g_ref, w_ref, mu_ref, cw_ref, sh0_ref, cv0_ref,
                   xs_ref, mb_ref, sa_ref, sh_ref, cv_ref, shc_s, cvc_s, *, tm, tps):
    i = pl.program_id(0)

    @pl.when(i % tps == 0)
    def _():
        shc_s[...] = sh0_ref[0]
        cvc_s[...] = cv0_ref[0]

    xb = _rmsnorm(x_ref[...], g_ref[...]).astype(BF16)
    row = lax.broadcasted_iota(jnp.int32, (tm, 1), 0)

    pc = _dot(xb, w_ref[:, DS:DS + 3 * D])
    u = pc[:, D:2 * D] * pc[:, 2 * D:3 * D]
    c0 = cvc_s[0:1, :]
    c1 = cvc_s[1:2, :]
    u1 = jnp.where(row == 0, c1, pltpu.roll(u, 1, 0))
    u2 = jnp.where(row == 0, c0, jnp.where(row == 1, c1, pltpu.roll(u, 2, 0)))
    z = cw_ref[0:1, :] * u2 + cw_ref[1:2, :] * u1 + cw_ref[2:3, :] * u
    pg = _dot(xb, w_ref[:, DS + 3 * D:DP])
    mb_ref[...] = _sigmoid(pg[:, D:2 * D]) * (pc[:, 0:D] * z)
    sa_ref[...] = _sigmoid(pg[:, 0:D])
    last_u = u[tm - 2:tm, :]
    cvc_s[...] = last_u

    ps = _dot(xb, w_ref[:, 0:DS])
    prev = jnp.where(row == 0, shc_s[...], pltpu.roll(ps, 1, 0))
    xs_ref[...] = ps + (prev - ps) * mu_ref[...]
    last_ps = ps[tm - 1:tm, :]
    shc_s[...] = last_ps

    @pl.when(i % tps == tps - 1)
    def _():
        sh_ref[0] = last_ps
        cv_ref[0] = last_u


def _proj_seq(x, seq_len, l, g, w, mu, cw, sh0, cv0):
    rows = x.shape[0]
    tm = min(PROJ_TILE, seq_len)
    assert rows % seq_len == 0 and seq_len % tm == 0 and tm >= 2
    nseq, tps = rows // seq_len, seq_len // tm
    return pl.pallas_call(
        functools.partial(_proj_seq_body, tm=tm, tps=tps),
        grid=(rows // tm,),
        in_specs=[
            pl.BlockSpec((tm, D), lambda i: (i, 0)),
            _layer_spec(g, l),
            _layer_spec(w, l, pipeline_mode=pl.Buffered(1)),
            _layer_spec(mu, l),
            _layer_spec(cw, l),
            _const_spec((1, 1, DS)),
            _const_spec((1, 2, D)),
        ],
        out_specs=[
            pl.BlockSpec((tm, DS), lambda i: (i, 0)),
            pl.BlockSpec((tm, D), lambda i: (i, 0)),
            pl.BlockSpec((tm, D), lambda i: (i, 0)),
            pl.BlockSpec((1, 1, DS), lambda i: (i // tps, 0, 0)),
            pl.BlockSpec((1, 2, D), lambda i: (i // tps, 0, 0)),
        ],
        out_shape=[
            jax.ShapeDtypeStruct((rows, DS), F32),
            jax.ShapeDtypeStruct((rows, D), F32),
            jax.ShapeDtypeStruct((rows, D), F32),
            jax.ShapeDtypeStruct((nseq, 1, DS), F32),
            jax.ShapeDtypeStruct((nseq, 2, D), F32),
        ],
        scratch_shapes=[pltpu.VMEM((1, DS), F32), pltpu.VMEM((2, D), F32)],
        compiler_params=pltpu.CompilerParams(
            dimension_semantics=("arbitrary",), vmem_limit_bytes=VMEM_LIMIT),
        name="proj_seq",
    )(x, g, w, mu, cw, sh0, cv0)


def _proj_step_body(x_ref, g_ref, w_ref, mu_ref, cw_ref, sh_ref, cv_ref,
                    xs_ref, mb_ref, sa_ref, sho_ref, cvo_ref):
    ps, pc, pg = _proj_columns(x_ref, g_ref, w_ref)
    xs_ref[...] = ps + (sh_ref[...] - ps) * mu_ref[...]
    sho_ref[...] = ps
    u = pc[:, D:2 * D] * pc[:, 2 * D:3 * D]
    c1 = cv_ref[:, 1, :]
    z = cw_ref[0:1, :] * cv_ref[:, 0, :] + cw_ref[1:2, :] * c1 + cw_ref[2:3, :] * u
    mb_ref[...] = _sigmoid(pg[:, D:2 * D]) * (pc[:, 0:D] * z)
    sa_ref[...] = _sigmoid(pg[:, 0:D])
    cvo_ref[:, 0, :] = c1
    cvo_ref[:, 1, :] = u


def _proj_step(x, l, g, w, mu, cw, sh_all, cv_all):
    rows = x.shape[0]
    return pl.pallas_call(
        _proj_step_body,
        grid=(1,),
        in_specs=[
            _const_spec((rows, D)),
            _layer_spec(g, l),
            _layer_spec(w, l, pipeline_mode=pl.Buffered(1)),
            _layer_spec(mu, l),
            _layer_spec(cw, l),
            _layer_spec(sh_all, l),
            _layer_spec(cv_all, l),
        ],
        out_specs=[
            _const_spec((rows, DS)),
            _const_spec((rows, D)),
            _const_spec((rows, D)),
            _const_spec((rows, DS)),
            _const_spec((rows, 2, D)),
        ],
        out_shape=[
            jax.ShapeDtypeStruct((rows, DS), F32),
            jax.ShapeDtypeStruct((rows, D), F32),
            jax.ShapeDtypeStruct((rows, D), F32),
            jax.ShapeDtypeStruct((rows, DS), F32),
            jax.ShapeDtypeStruct((rows, 2, D), F32),
        ],
        compiler_params=pltpu.CompilerParams(
            dimension_semantics=("arbitrary",), vmem_limit_bytes=VMEM_LIMIT),
        name="proj_step",
    )(x, g, w, mu, cw, sh_all, cv_all)


def _seg_sum(x, e):
    return jnp.concatenate([_dot(x[:, g * GL:(g + 1) * GL].astype(BF16), e) for g in range(NG)], axis=1)


def _mix_heads(xs, w0_ref, w2_ref, a0_ref, a2_ref, wg2_ref, kk_ref, ka_ref, rk_ref, e):
    r = xs(0, D)
    k = xs(D, 2 * D)
    v = xs(2 * D, 3 * D)
    lwla = xs(3 * D, 3 * D + LORA_WA)
    lg = xs(3 * D + LORA_WA, DS)
    wl = w0_ref[...] + _dot(jnp.tanh(lwla).astype(BF16), w2_ref[...])
    dexp = EXP_M05 * _sigmoid(wl)
    alr = _sigmoid(a0_ref[...] + _dot(lwla.astype(BF16), a2_ref[...]))
    gate = _dot(_sigmoid(lg).astype(BF16), wg2_ref[...])
    kkr = k * kk_ref[...]
    kk = kkr * lax.rsqrt(jnp.maximum(_seg_sum(kkr * kkr, e), 1e-24))
    kh = k * (1.0 + (alr - 1.0) * ka_ref[...])
    bonus = _seg_sum(r * kh * rk_ref[...], e) * v
    return r, kh, v, kk, alr, dexp, gate, bonus


def _mix_out(y, bonus, gate, sa, mb, x, lnw_ref, lnb_ref, wo_ref, e):
    mu = _seg_sum(y, e) * (1.0 / HD)
    d = y - mu
    var = _seg_sum(d * d, e) * (1.0 / HD)
    yn = d * lax.rsqrt(var + GN_EPS) * lnw_ref[...] + lnb_ref[...]
    m = sa * ((yn + bonus) * gate) + mb
    return x + _dot(m.astype(BF16), wo_ref[...])


def _mix_seq_body(xs_ref, mb_ref, sa_ref, x_ref, s0_ref,
                  w0_ref, w2_ref, a0_ref, a2_ref, wg2_ref, kk_ref, ka_ref, rk_ref,
                  lnw_ref, lnb_ref, wo_ref, e_ref,
                  o_ref, so_ref,
                  st_s, ah_s, rh_s, bt_s, kt_s, v_s, wl_s, y_s, *, tl, c):
    j = pl.program_id(1)
    gc = G * c
    nchunk = tl // c

    @pl.when(j == 0)
    def _():
        st_s[...] = jnp.zeros_like(st_s)
        for h in range(H):
            d0 = (h % G) * HD
            st_s[h // G, d0:d0 + HD, d0:d0 + HD] = s0_ref[0, h]

    e = e_ref[...]
    row = lax.broadcasted_iota(jnp.int32, (tl, 1), 0)
    r, kh, v, kk, alr, dexp, gate, bonus = _mix_heads(
        lambda a, b: xs_ref[0, :, a:b],
        w0_ref, w2_ref, a0_ref, a2_ref, wg2_ref, kk_ref, ka_ref, rk_ref, e)

    rin = row % c
    cum = dexp
    s = 1
    while s < c:
        cum = cum + jnp.where(rin >= s, pltpu.roll(cum, s, 0), 0.0)
        s *= 2
    wc = jnp.exp(-cum)
    winv = jnp.exp(cum)
    rh_s[...] = (r * wc).astype(BF16)
    ah_s[...] = (-kk * jnp.exp(dexp - cum)).astype(BF16)
    bt_s[...] = (kk * alr * winv).astype(BF16)
    kt_s[...] = (kh * winv).astype(BF16)
    v_s[...] = v.astype(BF16)
    for ci in range(nchunk):
        wl_s[ci:ci + 1, :] = wc[(ci + 1) * c - 1:(ci + 1) * c, :]

    groups = range(NG)
    lanes = [slice(g * GL, (g + 1) * GL) for g in groups]

    ri = lax.broadcasted_iota(jnp.int32, (gc, gc), 0)
    cj = lax.broadcasted_iota(jnp.int32, (gc, gc), 1)
    same = (ri // c) == (cj // c)
    strict = same & ((ri % c) > (cj % c))
    incl = same & ((ri % c) >= (cj % c))
    eye = (ri == cj).astype(F32)
    smask = (lax.broadcasted_iota(jnp.int32, (gc, GL), 0) // c) == (
        lax.broadcasted_iota(jnp.int32, (gc, GL), 1) // HD)

    def stack(x):
        return jnp.where(smask, jnp.concatenate([x] * G, axis=0), jnp.zeros((), x.dtype))

    def nt(a, b):
        return lax.dot_general(a, b, _NT, preferred_element_type=F32)

    def tn(a, b):
        return lax.dot_general(a, b, _TN, preferred_element_type=F32)

    merged = gc % LANES == 0 and 2 * gc <= MXU_DIM

    def state_free_stages(q):
        rs = slice(q * c, (q + 1) * c)
        t = {"rs": rs}

        def s_products_a():
            t["ah"] = [ah_s[rs, ls] for ls in lanes]
            t["bt"] = [bt_s[rs, ls] for ls in lanes]
            t["kt"] = [kt_s[rs, ls] for ls in lanes]
            a_st = [stack(x) for x in t["ah"]]
            b_rep = [jnp.concatenate([x] * G, axis=0) for x in t["bt"]]
            k_rep = [jnp.concatenate([x] * G, axis=0) for x in t["kt"]]
            if merged:
                t["rh"] = [rh_s[rs, ls] for ls in lanes]
                ar_st = [jnp.concatenate([a_st[g], stack(t["rh"][g])], axis=0) for g in groups]
                bk = [jnp.concatenate([b_rep[g], k_rep[g]], axis=0) for g in groups]
                gar = [nt(ar_st[g], bk[g]) for g in groups]
                l_ab = [jnp.where(strict, gar[g][:gc, :gc], 0.0) for g in groups]
                t["a_ak"] = [jnp.where(strict, gar[g][:gc, gc:], 0.0).astype(BF16) for g in groups]
                t["a_r"] = [jnp.concatenate([jnp.where(incl, gar[g][gc:, :gc], 0.0),
                                             jnp.where(incl, gar[g][gc:, gc:], 0.0)], axis=1).astype(BF16)
                            for g in groups]
            else:
                t["b_rep"], t["k_rep"] = b_rep, k_rep
                l_ab = [jnp.where(strict, nt(a_st[g], b_rep[g]), 0.0) for g in groups]
                t["a_ak"] = [jnp.where(strict, nt(a_st[g], k_rep[g]), 0.0).astype(BF16) for g in groups]
            t["lp"] = [x.astype(BF16) for x in l_ab]
            t["p"] = [eye + x for x in l_ab]

        def s_products_r():
            if not merged:
                t["rh"] = [rh_s[rs, ls] for ls in lanes]
                r_st = [stack(x) for x in t["rh"]]
                t["a_rb"] = [jnp.where(incl, nt(r_st[g], t["b_rep"][g]), 0.0).astype(BF16) for g in groups]
                t["a_rk"] = [jnp.where(incl, nt(r_st[g], t["k_rep"][g]), 0.0).astype(BF16) for g in groups]

        def s_av():
            t["v_st"] = [stack(v_s[rs, ls]) for ls in lanes]
            t["av"] = [_dot(t["a_ak"][g], t["v_st"][g]) for g in groups]

        def s_square():
            t["lp"] = [_dot(x, x).astype(BF16) for x in t["lp"]]

        def s_extend():
            t["p"] = [t["p"][g] + _dot(t["lp"][g], t["p"][g].astype(BF16)) for g in groups]

        def s_square_extend():
            both = [_dot(t["lp"][g], jnp.concatenate([t["lp"][g], t["p"][g].astype(BF16)], axis=1))
                    for g in groups]
            t["lp"] = [x[:, :gc].astype(BF16) for x in both]
            t["p"] = [t["p"][g] + both[g][:, gc:] for g in groups]

        def s_finish():
            t["pb"] = [x.astype(BF16) for x in t["p"]]
            t["b_st"] = [stack(x) for x in t["bt"]]
            t["k_st"] = [stack(x) for x in t["kt"]]

        stages = [s_products_a, s_products_r, s_av]
        factors = c.bit_length() - 1
        if merged:
            stages += [s_square] + [s_square_extend] * (factors - 2) + [s_extend]
        else:
            stages += [s_square, s_extend] * (factors - 1)
        return t, stages + [s_finish]

    def state_chain_stages(q, t, st):
        rs = t["rs"]
        w = {}

        def s_read_state():
            w["ars"] = [nt(jnp.concatenate([t["ah"][g], t["rh"][g]], axis=0), st[g].astype(BF16))
                        for g in groups]

        def s_solve():
            x_st = [stack(w["ars"][g][:c]) + t["av"][g] for g in groups]
            w["u"] = [_dot(t["pb"][g], x_st[g].astype(BF16)).astype(BF16) for g in groups]

        def s_outputs():
            if merged:
                uv = [jnp.concatenate([w["u"][g], t["v_st"][g]], axis=0) for g in groups]
                w["y"] = [_dot(t["a_r"][g], uv[g]) for g in groups]
                w["upd"] = [tn(uv[g], jnp.concatenate([t["b_st"][g], t["k_st"][g]], axis=0))
                            for g in groups]
            else:
                w["y"] = [_dot(t["a_rb"][g], w["u"][g]) + _dot(t["a_rk"][g], t["v_st"][g]) for g in groups]
                w["upd"] = [tn(w["u"][g], t["b_st"][g]) + tn(t["v_st"][g], t["k_st"][g]) for g in groups]

        def s_commit():
            for g in groups:
                y = w["ars"][g][c:]
                for hh in range(G):
                    y = y + w["y"][g][hh * c:(hh + 1) * c]
                y_s[rs, lanes[g]] = y
                st[g] = (st[g] + w["upd"][g]) * wl_s[q:q + 1, lanes[g]]

        return [s_read_state, s_solve, s_outputs, s_commit]

    st = [st_s[g] for g in groups]
    cur_t, first = state_free_stages(0)
    for stage in first:
        stage()
    for q in range(nchunk):
        chain = state_chain_stages(q, cur_t, st)
        nxt_t, free = state_free_stages(q + 1) if q + 1 < nchunk else (None, [])
        every = max(1, len(free) // len(chain))
        fi = 0
        for stage in chain:
            for _ in range(every):
                if fi < len(free):
                    free[fi]()
                    fi += 1
            stage()
        while fi < len(free):
            free[fi]()
            fi += 1
        cur_t = nxt_t
    for g in groups:
        st_s[g] = st[g]

    o_ref[0] = _mix_out(y_s[...], bonus, gate, sa_ref[0], mb_ref[0], x_ref[0],
                        lnw_ref, lnb_ref, wo_ref, e)

    @pl.when(j == pl.num_programs(1) - 1)
    def _():
        for h in range(H):
            d0 = (h % G) * HD
            so_ref[0, h] = st_s[h // G, d0:d0 + HD, d0:d0 + HD]


def _mix_seq(xs, mb, sa, x, s0, l, wts, e, *, tl, c):
    bsz, length, _ = x.shape
    assert length % tl == 0 and tl % c == 0
    nchunk = tl // c

    def tile(width):
        return pl.BlockSpec((1, tl, width), lambda b, j: (b, j, 0))

    return pl.pallas_call(
        functools.partial(_mix_seq_body, tl=tl, c=c),
        grid=(bsz, length // tl),
        in_specs=[tile(DS), tile(D), tile(D), tile(D), _const_spec((1, H, HD, HD))]
        + [_layer_spec(w, l) for w in wts] + [_const_spec(e.shape)],
        out_specs=[tile(D), pl.BlockSpec((1, H, HD, HD), lambda b, j: (b, 0, 0, 0))],
        out_shape=[
            jax.ShapeDtypeStruct((bsz, length, D), F32),
            jax.ShapeDtypeStruct((bsz, H, HD, HD), F32),
        ],
        scratch_shapes=[pltpu.VMEM((NG, GL, GL), F32)] + [pltpu.VMEM((tl, D), BF16)] * 5 + [
            pltpu.VMEM((max(8, nchunk), D), F32),
            pltpu.VMEM((tl, D), F32),
        ],
        compiler_params=pltpu.CompilerParams(
            dimension_semantics=("arbitrary", "arbitrary"), vmem_limit_bytes=VMEM_LIMIT),
        name=f"mix_seq_{tl}",
    )(xs, mb, sa, x, s0, *wts, e)


def _mix_step_body(xs_ref, mb_ref, sa_ref, x_ref, s_ref,
                   w0_ref, w2_ref, a0_ref, a2_ref, wg2_ref, kk_ref, ka_ref, rk_ref,
                   lnw_ref, lnb_ref, wo_ref, e_ref,
                   o_ref, so_ref,
                   a_s, wr_s, w_s, b_s, k_s, v_s, s1_s, s2_s, z_s, *, bt):
    e = e_ref[...]
    r, kh, v, kk, alr, dexp, gate, bonus = _mix_heads(
        lambda a, b: xs_ref[:, a:b],
        w0_ref, w2_ref, a0_ref, a2_ref, wg2_ref, kk_ref, ka_ref, rk_ref, e)
    w = jnp.exp(-dexp)
    bb = kk * alr
    lane = lax.broadcasted_iota(jnp.int32, (bt, LANES), 1)

    def fill(dst, val):
        for p in range(H // 2):
            blk = val[:, p * LANES:(p + 1) * LANES]
            rolled = pltpu.roll(blk, HD, 1)
            dst[2 * p] = jnp.where(lane < HD, blk, rolled)
            dst[2 * p + 1] = jnp.where(lane < HD, rolled, blk)

    fill(a_s, -kk)
    fill(wr_s, w * r)
    fill(w_s, w)
    fill(b_s, bb)
    fill(k_s, kh)
    fill(v_s, v)
    fill(s1_s, _seg_sum(bb * r, e))
    fill(s2_s, _seg_sum(kh * r, e))

    li = lax.broadcasted_iota(jnp.int32, (LANES, LANES), 0)
    lj = lax.broadcasted_iota(jnp.int32, (LANES, LANES), 1)
    half_sum = ((li // HD) == (lj // HD)).astype(BF16)
    half_sum2 = jnp.concatenate([half_sum, half_sum], axis=0)
    src_v = 2 * (li % HD) + li // HD
    perm_lo = ((li % HD < PK) & (lj == src_v)).astype(BF16)
    perm_hi = ((li % HD < PK) & (lj == src_v + HD)).astype(BF16)
    pi = lax.broadcasted_iota(jnp.int32, (PK, LANES), 0)
    pj = lax.broadcasted_iota(jnp.int32, (PK, LANES), 1)
    pick_v = ((pj % HD) == 2 * pi + pj // HD).astype(F32)
    diag = ((pj % HD) == pi).astype(F32)

    def seq_step(b, carry):
        rb = pl.ds(b, 1)
        xa, xw, xv = [], [], []
        for h in range(H):
            s = s_ref[b, h]
            xa.append((s * a_s[h, rb, :]).astype(BF16))
            xw.append((s * wr_s[h, rb, :]).astype(BF16))
            vm = pick_v * v_s[h, rb, :]
            vhi = vm.astype(BF16)
            xv.append(jnp.concatenate([vhi, (vm - vhi.astype(F32)).astype(BF16)], axis=1))
        sa = _dot(jnp.concatenate(xa, axis=0), half_sum)
        swr = _dot(jnp.concatenate(xw, axis=0), half_sum)
        vb = _dot(jnp.concatenate(xv, axis=0), half_sum2)
        for h in range(H):
            hs = slice(h * PK, (h + 1) * PK)
            so_ref[b, h] = (s_ref[b, h] * w_s[h, rb, :] + sa[hs] * b_s[h, rb, :]
                            + vb[hs] * k_s[h, rb, :])
            y = swr[hs] + sa[hs] * s1_s[h, rb, :] + vb[hs] * s2_s[h, rb, :]
            z_s[h, rb, :] = jnp.sum(y * diag, axis=0, keepdims=True)
        return carry

    lax.fori_loop(0, bt, seq_step, 0)
    y = jnp.concatenate(
        [_dot_split(z_s[2 * p], perm_lo) + _dot_split(z_s[2 * p + 1], perm_hi) for p in range(H // 2)],
        axis=1)
    o_ref[...] = _mix_out(y, bonus, gate, sa_ref[...], mb_ref[...], x_ref[...],
                          lnw_ref, lnb_ref, wo_ref, e)


def _mix_step(xs, mb, sa, x, states, l, wts, e):
    bsz = x.shape[0]
    bt = STEP_TILE
    assert bsz % bt == 0
    state_spec = pl.BlockSpec((None, bt, H, PK, LANES), lambda i: (l, i, 0, 0, 0))

    def rows(width):
        return pl.BlockSpec((bt, width), lambda i: (i, 0))

    tokens = (xs, mb, sa, x)
    return pl.pallas_call(
        functools.partial(_mix_step_body, bt=bt),
        grid=(bsz // bt,),
        in_specs=[rows(DS), rows(D), rows(D), rows(D), state_spec]
        + [_layer_spec(w, l) for w in wts] + [_const_spec(e.shape)],
        out_specs=[rows(D), state_spec],
        out_shape=[
            jax.ShapeDtypeStruct((bsz, D), F32),
            jax.ShapeDtypeStruct(states.shape, F32),
        ],
        scratch_shapes=[pltpu.VMEM((H, bt, LANES), F32)] * 9,
        input_output_aliases={len(tokens): 1},
        compiler_params=pltpu.CompilerParams(
            dimension_semantics=("arbitrary",), vmem_limit_bytes=VMEM_LIMIT),
        name="mix_step",
    )(*tokens, states, *wts, e)


def kernel(x_prompt, x_sample, state_wkv, state_shift, state_conv, meta_tokens, norm_ffn1, ffn1_w_in, ffn1_w_out, norm_mix, w_in, mu_shift, w0, w_w2, a0, w_a2, w_g2, k_k, k_a, r_k, lnx_w, lnx_b, conv_w, w_o, norm_ffn2, ffn2_w_in, ffn2_w_out, norm_final):
    depth = norm_ffn1.shape[0]
    bp, lp, _ = x_prompt.shape
    bs = x_sample.shape[0]
    n_meta = meta_tokens.shape[0]

    def vec(a):
        return a.reshape(depth, 1, -1)

    half_zeros = jnp.zeros((depth, LORA_WA // 2, D), F32)
    wts = tuple(dict(
        w0=vec(w0),
        w2=jnp.concatenate([w_w2, half_zeros], axis=1).astype(BF16),
        a0=vec(a0),
        a2=jnp.concatenate([half_zeros, w_a2], axis=1).astype(BF16),
        wg2=w_g2.astype(BF16), k_k=vec(k_k), k_a=vec(k_a), r_k=vec(r_k), lnw=vec(lnx_w),
        lnb=vec(lnx_b), wo=w_o.astype(BF16))[name] for name in _MIX_WEIGHTS)
    lane = jnp.arange(GL)
    e = (lane[:, None] // HD == lane[None, :] // HD).astype(BF16)
    n1, nm, n2, mu = vec(norm_ffn1), vec(norm_mix), vec(norm_ffn2), vec(mu_shift)
    f1_in, f1_out = ffn1_w_in.astype(BF16), ffn1_w_out.astype(BF16)
    f2_in, f2_out = ffn2_w_in.astype(BF16), ffn2_w_out.astype(BF16)
    w_in_b = w_in.astype(BF16)
    gfin = norm_final[None]

    xm = meta_tokens.astype(x_prompt.dtype)
    xp = x_prompt.reshape(bp * lp, D)
    xs = x_sample.reshape(bs, D)
    states = state_wkv.reshape(depth, bs, H, PK, LANES)
    zero_s = jnp.zeros((1, H, HD, HD), F32)
    zero_sh = jnp.zeros((1, 1, DS), F32)
    zero_cv = jnp.zeros((1, 2, D), F32)
    p_wkv, p_shift, p_conv, s_shift, s_conv = [], [], [], [], []
    def ffn_meta_sample(xm, xs, l, g, w1, w2):
        y = _ffn(jnp.concatenate([xm, xs], axis=0), l, g, w1, w2, gfin, False)
        return y[:n_meta], y[n_meta:]

    for l in range(depth):
        xm, xs = ffn_meta_sample(xm, xs, l, n1, f1_in, f1_out)
        xp = _ffn(xp, l, n1, f1_in, f1_out, gfin, False)
        m_xs, m_mb, m_sa, m_sh, m_cv = _proj_seq(xm, n_meta, l, nm, w_in_b, mu, conv_w, zero_sh, zero_cv)
        p_xs, p_mb, p_sa, p_sh, p_cv = _proj_seq(xp, lp, l, nm, w_in_b, mu, conv_w, m_sh, m_cv)
        s_xs, s_mb, s_sa, s_sh, s_cv = _proj_step(xs, l, nm, w_in_b, mu, conv_w, state_shift, state_conv)
        xm3, m_s = _mix_seq(m_xs.reshape(1, n_meta, DS), m_mb.reshape(1, n_meta, D),
                            m_sa.reshape(1, n_meta, D), xm.reshape(1, n_meta, D), zero_s,
                            l, wts, e, tl=n_meta, c=n_meta)
        xp3, p_s = _mix_seq(p_xs.reshape(bp, lp, DS), p_mb.reshape(bp, lp, D), p_sa.reshape(bp, lp, D),
                            xp.reshape(bp, lp, D), m_s, l, wts, e, tl=min(SEQ_TILE, lp), c=CHUNK)
        xs, states = _mix_step(s_xs, s_mb, s_sa, xs, states, l, wts, e)
        xm = xm3.reshape(n_meta, D)
        xp = xp3.reshape(bp * lp, D)
        last = l == depth - 1
        if last:
            xs = _ffn(xs, l, n2, f2_in, f2_out, gfin, True)
        else:
            xm, xs = ffn_meta_sample(xm, xs, l, n2, f2_in, f2_out)
        xp = _ffn(xp, l, n2, f2_in, f2_out, gfin, last)
        p_wkv.append(p_s)
        p_shift.append(p_sh[:, 0])
        p_conv.append(p_cv)
        s_shift.append(s_sh)
        s_conv.append(s_cv)
    return (xp.reshape(bp, lp, D), xs.reshape(bs, 1, D), jnp.stack(p_wkv), jnp.stack(p_shift),
            jnp.stack(p_conv), states.reshape(state_wkv.shape), jnp.stack(s_shift), jnp.stack(s_conv))
```

```python
import functools
import math

import jax
import jax.numpy as jnp
from jax import lax
from jax.experimental import pallas as pl
from jax.experimental.pallas import tpu as pltpu

F32 = jnp.float32
BF16 = jnp.bfloat16

D = 1024
H = 16
HD = 64
G = 4
GL = G * HD
NG = H // G
LORA_WA = 128
LORA_G = 128
DS = 3 * D + LORA_WA + LORA_G
DP = DS + 3 * D + 2 * D
DFF = 2816
RMS_EPS = 1e-6
GN_EPS = 64e-5
EXP_M05 = math.exp(-0.5)
LANES = 128
MXU_DIM = 256
PK = HD * HD // LANES

CHUNK = 32
SEQ_TILE = 512
ROW_TILE = 512
PROJ_TILE = 256
STEP_TILE = 32
VMEM_LIMIT = 56 * 1024 * 1024

_NT = (((1,), (1,)), ((), ()))
_TN = (((0,), (0,)), ((), ()))

_MIX_WEIGHTS = ("w0", "w2", "a0", "a2", "wg2", "k_k", "k_a", "r_k", "lnw", "lnb", "wo")


def _dot(a, b):
    return jnp.dot(a, b, preferred_element_type=F32)


def _dot_split(x, m):
    hi = x.astype(BF16)
    lo = (x - hi.astype(F32)).astype(BF16)
    return _dot(hi, m) + _dot(lo, m)


def _rmsnorm(x, g):
    return x * lax.rsqrt(jnp.mean(x * x, axis=-1, keepdims=True) + RMS_EPS) * g


def _sigmoid(x):
    return 1.0 / (1.0 + jnp.exp(-x))


def _const_spec(shape):
    nd = len(shape)
    return pl.BlockSpec(shape, lambda *_: (0,) * nd)


def _layer_spec(arr, l, **kw):
    nd = arr.ndim - 1
    return pl.BlockSpec((None,) + arr.shape[1:], lambda *_: (l,) + (0,) * nd, **kw)


def _ffn_body(x_ref, g_ref, win_ref, wout_ref, gf_ref, o_ref, *, final_norm):
    x = x_ref[...]
    xb = _rmsnorm(x, g_ref[...]).astype(BF16)
    gate = _dot(xb, win_ref[:, 0:DFF])
    up = _dot(xb, win_ref[:, DFF:2 * DFF])
    act = (gate * _sigmoid(gate) * up).astype(BF16)
    y = x + 0.5 * _dot(act, wout_ref[...])
    if final_norm:
        y = _rmsnorm(y, gf_ref[...])
    o_ref[...] = y


def _ffn(x, l, g, w_in, w_out, g_final, final_norm):
    rows = x.shape[0]
    tm = min(ROW_TILE, rows)
    assert rows % tm == 0
    return pl.pallas_call(
        functools.partial(_ffn_body, final_norm=final_norm),
        grid=(rows // tm,),
        in_specs=[
            pl.BlockSpec((tm, D), lambda i: (i, 0)),
            _layer_spec(g, l),
            _layer_spec(w_in, l, pipeline_mode=pl.Buffered(1)),
            _layer_spec(w_out, l, pipeline_mode=pl.Buffered(1)),
            _const_spec((1, D)),
        ],
        out_specs=pl.BlockSpec((tm, D), lambda i: (i, 0)),
        out_shape=jax.ShapeDtypeStruct((rows, D), F32),
        compiler_params=pltpu.CompilerParams(
            dimension_semantics=("arbitrary",), vmem_limit_bytes=VMEM_LIMIT),
        name="ffn",
    )(x, g, w_in, w_out, g_final)


def _proj_columns(x_ref, g_ref, w_ref):
    xb = _rmsnorm(x_ref[...], g_ref[...]).astype(BF16)
    ps = _dot(xb, w_ref[:, 0:DS])
    pc = _dot(xb, w_ref[:, DS:DS + 3 * D])
    pg = _dot(xb, w_ref[:, DS + 3 * D:DP])
    return ps, pc, pg


def _proj_seq_body(x_ref, g_ref, w_ref, mu_ref, cw_ref, sh0_ref, cv0_ref,
                   xs_ref, mb_ref, sa_ref, sh_ref, cv_ref, shc_s, cvc_s, *, tm, tps):
    i = pl.program_id(0)

    @pl.when(i % tps == 0)
    def _():
        shc_s[...] = sh0_ref[0]
        cvc_s[...] = cv0_ref[0]

    xb = _rmsnorm(x_ref[...], g_ref[...]).astype(BF16)
    row = lax.broadcasted_iota(jnp.int32, (tm, 1), 0)

    pc = _dot(xb, w_ref[:, DS:DS + 3 * D])
    u = pc[:, D:2 * D] * pc[:, 2 * D:3 * D]
    c0 = cvc_s[0:1, :]
    c1 = cvc_s[1:2, :]
    u1 = jnp.where(row == 0, c1, pltpu.roll(u, 1, 0))
    u2 = jnp.where(row == 0, c0, jnp.where(row == 1, c1, pltpu.roll(u, 2, 0)))
    z = cw_ref[0:1, :] * u2 + cw_ref[1:2, :] * u1 + cw_ref[2:3, :] * u
    pg = _dot(xb, w_ref[:, DS + 3 * D:DP])
    mb_ref[...] = _sigmoid(pg[:, D:2 * D]) * (pc[:, 0:D] * z)
    sa_ref[...] = _sigmoid(pg[:, 0:D])
    last_u = u[tm - 2:tm, :]
    cvc_s[...] = last_u

    ps = _dot(xb, w_ref[:, 0:DS])
    prev = jnp.where(row == 0, shc_s[...], pltpu.roll(ps, 1, 0))
    xs_ref[...] = ps + (prev - ps) * mu_ref[...]
    last_ps = ps[tm - 1:tm, :]
    shc_s[...] = last_ps

    @pl.when(i % tps == tps - 1)
    def _():
        sh_ref[0] = last_ps
        cv_ref[0] = last_u


def _proj_seq(x, seq_len, l, g, w, mu, cw, sh0, cv0):
    rows = x.shape[0]
    tm = min(PROJ_TILE, seq_len)
    assert rows % seq_len == 0 and seq_len % tm == 0 and tm >= 2
    nseq, tps = rows // seq_len, seq_len // tm
    return pl.pallas_call(
        functools.partial(_proj_seq_body, tm=tm, tps=tps),
        grid=(rows // tm,),
        in_specs=[
            pl.BlockSpec((tm, D), lambda i: (i, 0)),
            _layer_spec(g, l),
            _layer_spec(w, l, pipeline_mode=pl.Buffered(1)),
            _layer_spec(mu, l),
            _layer_spec(cw, l),
            _const_spec((1, 1, DS)),
            _const_spec((1, 2, D)),
        ],
        out_specs=[
            pl.BlockSpec((tm, DS), lambda i: (i, 0)),
            pl.BlockSpec((tm, D), lambda i: (i, 0)),
            pl.BlockSpec((tm, D), lambda i: (i, 0)),
            pl.BlockSpec((1, 1, DS), lambda i: (i // tps, 0, 0)),
            pl.BlockSpec((1, 2, D), lambda i: (i // tps, 0, 0)),
        ],
        out_shape=[
            jax.ShapeDtypeStruct((rows, DS), F32),
            jax.ShapeDtypeStruct((rows, D), F32),
            jax.ShapeDtypeStruct((rows, D), F32),
            jax.ShapeDtypeStruct((nseq, 1, DS), F32),
            jax.ShapeDtypeStruct((nseq, 2, D), F32),
        ],
        scratch_shapes=[pltpu.VMEM((1, DS), F32), pltpu.VMEM((2, D), F32)],
        compiler_params=pltpu.CompilerParams(
            dimension_semantics=("arbitrary",), vmem_limit_bytes=VMEM_LIMIT),
        name="proj_seq",
    )(x, g, w, mu, cw, sh0, cv0)


def _proj_step_body(x_ref, g_ref, w_ref, mu_ref, cw_ref, sh_ref, cv_ref,
                    xs_ref, mb_ref, sa_ref, sho_ref, cvo_ref):
    ps, pc, pg = _proj_columns(x_ref, g_ref, w_ref)
    xs_ref[...] = ps + (sh_ref[...] - ps) * mu_ref[...]
    sho_ref[...] = ps
    u = pc[:, D:2 * D] * pc[:, 2 * D:3 * D]
    c1 = cv_ref[:, 1, :]
    z = cw_ref[0:1, :] * cv_ref[:, 0, :] + cw_ref[1:2, :] * c1 + cw_ref[2:3, :] * u
    mb_ref[...] = _sigmoid(pg[:, D:2 * D]) * (pc[:, 0:D] * z)
    sa_ref[...] = _sigmoid(pg[:, 0:D])
    cvo_ref[:, 0, :] = c1
    cvo_ref[:, 1, :] = u


def _proj_step(x, l, g, w, mu, cw, sh_all, cv_all):
    rows = x.shape[0]
    return pl.pallas_call(
        _proj_step_body,
        grid=(1,),
        in_specs=[
            _const_spec((rows, D)),
            _layer_spec(g, l),
            _layer_spec(w, l, pipeline_mode=pl.Buffered(1)),
            _layer_spec(mu, l),
            _layer_spec(cw, l),
            _layer_spec(sh_all, l),
            _layer_spec(cv_all, l),
        ],
        out_specs=[
            _const_spec((rows, DS)),
            _const_spec((rows, D)),
            _const_spec((rows, D)),
            _const_spec((rows, DS)),
            _const_spec((rows, 2, D)),
        ],
        out_shape=[
            jax.ShapeDtypeStruct((rows, DS), F32),
            jax.ShapeDtypeStruct((rows, D), F32),
            jax.ShapeDtypeStruct((rows, D), F32),
            jax.ShapeDtypeStruct((rows, DS), F32),
            jax.ShapeDtypeStruct((rows, 2, D), F32),
        ],
        compiler_params=pltpu.CompilerParams(
            dimension_semantics=("arbitrary",), vmem_limit_bytes=VMEM_LIMIT),
        name="proj_step",
    )(x, g, w, mu, cw, sh_all, cv_all)


def _seg_sum(x, e):
    return jnp.concatenate([_dot(x[:, g * GL:(g + 1) * GL].astype(BF16), e) for g in range(NG)], axis=1)


def _mix_heads(xs, w0_ref, w2_ref, a0_ref, a2_ref, wg2_ref, kk_ref, ka_ref, rk_ref, e):
    r = xs(0, D)
    k = xs(D, 2 * D)
    v = xs(2 * D, 3 * D)
    lwla = xs(3 * D, 3 * D + LORA_WA)
    lg = xs(3 * D + LORA_WA, DS)
    wl = w0_ref[...] + _dot(jnp.tanh(lwla).astype(BF16), w2_ref[...])
    dexp = EXP_M05 * _sigmoid(wl)
    alr = _sigmoid(a0_ref[...] + _dot(lwla.astype(BF16), a2_ref[...]))
    gate = _dot(_sigmoid(lg).astype(BF16), wg2_ref[...])
    kkr = k * kk_ref[...]
    kk = kkr * lax.rsqrt(jnp.maximum(_seg_sum(kkr * kkr, e), 1e-24))
    kh = k * (1.0 + (alr - 1.0) * ka_ref[...])
    bonus = _seg_sum(r * kh * rk_ref[...], e) * v
    return r, kh, v, kk, alr, dexp, gate, bonus


def _mix_out(y, bonus, gate, sa, mb, x, lnw_ref, lnb_ref, wo_ref, e):
    mu = _seg_sum(y, e) * (1.0 / HD)
    d = y - mu
    var = _seg_sum(d * d, e) * (1.0 / HD)
    yn = d * lax.rsqrt(var + GN_EPS) * lnw_ref[...] + lnb_ref[...]
    m = sa * ((yn + bonus) * gate) + mb
    return x + _dot(m.astype(BF16), wo_ref[...])


def _mix_seq_body(xs_ref, mb_ref, sa_ref, x_ref, s0_ref,
                  w0_ref, w2_ref, a0_ref, a2_ref, wg2_ref, kk_ref, ka_ref, rk_ref,
                  lnw_ref, lnb_ref, wo_ref, e_ref,
                  o_ref, so_ref,
                  st_s, ah_s, rh_s, bt_s, kt_s, v_s, wl_s, y_s, *, tl, c):
    j = pl.program_id(1)
    gc = G * c
    nchunk = tl // c

    @pl.when(j == 0)
    def _():
        st_s[...] = jnp.zeros_like(st_s)
        for h in range(H):
            d0 = (h % G) * HD
            st_s[h // G, d0:d0 + HD, d0:d0 + HD] = s0_ref[0, h]

    e = e_ref[...]
    row = lax.broadcasted_iota(jnp.int32, (tl, 1), 0)
    r, kh, v, kk, alr, dexp, gate, bonus = _mix_heads(
        lambda a, b: xs_ref[0, :, a:b],
        w0_ref, w2_ref, a0_ref, a2_ref, wg2_ref, kk_ref, ka_ref, rk_ref, e)

    rin = row % c
    cum = dexp
    s = 1
    while s < c:
        cum = cum + jnp.where(rin >= s, pltpu.roll(cum, s, 0), 0.0)
        s *= 2
    wc = jnp.exp(-cum)
    winv = jnp.exp(cum)
    rh_s[...] = (r * wc).astype(BF16)
    ah_s[...] = (-kk * jnp.exp(dexp - cum)).astype(BF16)
    bt_s[...] = (kk * alr * winv).astype(BF16)
    kt_s[...] = (kh * winv).astype(BF16)
    v_s[...] = v.astype(BF16)
    for ci in range(nchunk):
        wl_s[ci:ci + 1, :] = wc[(ci + 1) * c - 1:(ci + 1) * c, :]

    groups = range(NG)
    lanes = [slice(g * GL, (g + 1) * GL) for g in groups]

    ri = lax.broadcasted_iota(jnp.int32, (gc, gc), 0)
    cj = lax.broadcasted_iota(jnp.int32, (gc, gc), 1)
    same = (ri // c) == (cj // c)
    strict = same & ((ri % c) > (cj % c))
    incl = same & ((ri % c) >= (cj % c))
    eye = (ri == cj).astype(F32)
    smask = (lax.broadcasted_iota(jnp.int32, (gc, GL), 0) // c) == (
        lax.broadcasted_iota(jnp.int32, (gc, GL), 1) // HD)

    def stack(x):
        return jnp.where(smask, jnp.concatenate([x] * G, axis=0), jnp.zeros((), x.dtype))

    def nt(a, b):
        return lax.dot_general(a, b, _NT, preferred_element_type=F32)

    def tn(a, b):
        return lax.dot_general(a, b, _TN, preferred_element_type=F32)

    merged = gc % LANES == 0 and 2 * gc <= MXU_DIM

    def state_free_stages(q):
        rs = slice(q * c, (q + 1) * c)
        t = {"rs": rs}

        def s_products_a():
            t["ah"] = [ah_s[rs, ls] for ls in lanes]
            t["bt"] = [bt_s[rs, ls] for ls in lanes]
            t["kt"] = [kt_s[rs, ls] for ls in lanes]
            a_st = [stack(x) for x in t["ah"]]
            b_rep = [jnp.concatenate([x] * G, axis=0) for x in t["bt"]]
            k_rep = [jnp.concatenate([x] * G, axis=0) for x in t["kt"]]
            if merged:
                t["rh"] = [rh_s[rs, ls] for ls in lanes]
                ar_st = [jnp.concatenate([a_st[g], stack(t["rh"][g])], axis=0) for g in groups]
                bk = [jnp.concatenate([b_rep[g], k_rep[g]], axis=0) for g in groups]
                gar = [nt(ar_st[g], bk[g]) for g in groups]
                l_ab = [jnp.where(strict, gar[g][:gc, :gc], 0.0) for g in groups]
                t["a_ak"] = [jnp.where(strict, gar[g][:gc, gc:], 0.0).astype(BF16) for g in groups]
                t["a_r"] = [jnp.concatenate([jnp.where(incl, gar[g][gc:, :gc], 0.0),
                                             jnp.where(incl, gar[g][gc:, gc:], 0.0)], axis=1).astype(BF16)
                            for g in groups]
            else:
                t["b_rep"], t["k_rep"] = b_rep, k_rep
                l_ab = [jnp.where(strict, nt(a_st[g], b_rep[g]), 0.0) for g in groups]
                t["a_ak"] = [jnp.where(strict, nt(a_st[g], k_rep[g]), 0.0).astype(BF16) for g in groups]
            t["lp"] = [x.astype(BF16) for x in l_ab]
            t["p"] = [eye + x for x in l_ab]

        def s_products_r():
            if not merged:
                t["rh"] = [rh_s[rs, ls] for ls in lanes]
                r_st = [stack(x) for x in t["rh"]]
                t["a_rb"] = [jnp.where(incl, nt(r_st[g], t["b_rep"][g]), 0.0).astype(BF16) for g in groups]
                t["a_rk"] = [jnp.where(incl, nt(r_st[g], t["k_rep"][g]), 0.0).astype(BF16) for g in groups]

        def s_av():
            t["v_st"] = [stack(v_s[rs, ls]) for ls in lanes]
            t["av"] = [_dot(t["a_ak"][g], t["v_st"][g]) for g in groups]

        def s_square():
            t["lp"] = [_dot(x, x).astype(BF16) for x in t["lp"]]

        def s_extend():
            t["p"] = [t["p"][g] + _dot(t["lp"][g], t["p"][g].astype(BF16)) for g in groups]

        def s_square_extend():
            both = [_dot(t["lp"][g], jnp.concatenate([t["lp"][g], t["p"][g].astype(BF16)], axis=1))
                    for g in groups]
            t["lp"] = [x[:, :gc].astype(BF16) for x in both]
            t["p"] = [t["p"][g] + both[g][:, gc:] for g in groups]

        def s_finish():
            t["pb"] = [x.astype(BF16) for x in t["p"]]
            t["b_st"] = [stack(x) for x in t["bt"]]
            t["k_st"] = [stack(x) for x in t["kt"]]

        stages = [s_products_a, s_products_r]
        factors = c.bit_length() - 1
        if merged:
            stages += [s_square] + [s_square_extend] * (factors - 2) + [s_extend]
        else:
            stages += [s_square, s_extend] * (factors - 1)
        return t, stages + [s_av, s_finish]

    def state_chain_stages(q, t, st):
        rs = t["rs"]
        w = {}

        def s_read_state():
            w["ars"] = [nt(jnp.concatenate([t["ah"][g], t["rh"][g]], axis=0), st[g].astype(BF16))
                        for g in groups]

        def s_solve():
            x_st = [stack(w["ars"][g][:c]) + t["av"][g] for g in groups]
            w["u"] = [_dot(t["pb"][g], x_st[g].astype(BF16)).astype(BF16) for g in groups]

        def s_outputs():
            if merged:
                uv = [jnp.concatenate([w["u"][g], t["v_st"][g]], axis=0) for g in groups]
                w["y"] = [_dot(t["a_r"][g], uv[g]) for g in groups]
                w["upd"] = [tn(uv[g], jnp.concatenate([t["b_st"][g], t["k_st"][g]], axis=0))
                            for g in groups]
            else:
                w["y"] = [_dot(t["a_rb"][g], w["u"][g]) + _dot(t["a_rk"][g], t["v_st"][g]) for g in groups]
                w["upd"] = [tn(w["u"][g], t["b_st"][g]) + tn(t["v_st"][g], t["k_st"][g]) for g in groups]

        def s_commit():
            for g in groups:
                y = w["ars"][g][c:]
                for hh in range(G):
                    y = y + w["y"][g][hh * c:(hh + 1) * c]
                y_s[rs, lanes[g]] = y
                st[g] = (st[g] + w["upd"][g]) * wl_s[q:q + 1, lanes[g]]

        return [s_read_state, s_solve, s_outputs, s_commit]

    st = [st_s[g] for g in groups]
    cur_t, first = state_free_stages(0)
    for stage in first:
        stage()
    for q in range(nchunk):
        chain = state_chain_stages(q, cur_t, st)
        nxt_t, free = state_free_stages(q + 1) if q + 1 < nchunk else (None, [])
        every = max(1, len(free) // len(chain))
        fi = 0
        for stage in chain:
            for _ in range(every):
                if fi < len(free):
                    free[fi]()
                    fi += 1
            stage()
        while fi < len(free):
            free[fi]()
            fi += 1
        cur_t = nxt_t
    for g in groups:
        st_s[g] = st[g]

    o_ref[0] = _mix_out(y_s[...], bonus, gate, sa_ref[0], mb_ref[0], x_ref[0],
                        lnw_ref, lnb_ref, wo_ref, e)

    @pl.when(j == pl.num_programs(1) - 1)
    def _():
        for h in range(H):
            d0 = (h % G) * HD
            so_ref[0, h] = st_s[h // G, d0:d0 + HD, d0:d0 + HD]


def _mix_seq(xs, mb, sa, x, s0, l, wts, e, *, tl, c):
    bsz, length, _ = x.shape
    assert length % tl == 0 and tl % c == 0
    nchunk = tl // c

    def tile(width):
        return pl.BlockSpec((1, tl, width), lambda b, j: (b, j, 0))

    return pl.pallas_call(
        functools.partial(_mix_seq_body, tl=tl, c=c),
        grid=(bsz, length // tl),
        in_specs=[tile(DS), tile(D), tile(D), tile(D), _const_spec((1, H, HD, HD))]
        + [_layer_spec(w, l) for w in wts] + [_const_spec(e.shape)],
        out_specs=[tile(D), pl.BlockSpec((1, H, HD, HD), lambda b, j: (b, 0, 0, 0))],
        out_shape=[
            jax.ShapeDtypeStruct((bsz, length, D), F32),
            jax.ShapeDtypeStruct((bsz, H, HD, HD), F32),
        ],
        scratch_shapes=[pltpu.VMEM((NG, GL, GL), F32)] + [pltpu.VMEM((tl, D), BF16)] * 5 + [
            pltpu.VMEM((max(8, nchunk), D), F32),
            pltpu.VMEM((tl, D), F32),
        ],
        compiler_params=pltpu.CompilerParams(
            dimension_semantics=("arbitrary", "arbitrary"), vmem_limit_bytes=VMEM_LIMIT),
        name=f"mix_seq_{tl}",
    )(xs, mb, sa, x, s0, *wts, e)


def _mix_step_body(xs_ref, mb_ref, sa_ref, x_ref, s_ref,
                   w0_ref, w2_ref, a0_ref, a2_ref, wg2_ref, kk_ref, ka_ref, rk_ref,
                   lnw_ref, lnb_ref, wo_ref, e_ref,
                   o_ref, so_ref,
                   a_s, wr_s, w_s, b_s, k_s, v_s, s1_s, s2_s, z_s, *, bt):
    e = e_ref[...]
    r, kh, v, kk, alr, dexp, gate, bonus = _mix_heads(
        lambda a, b: xs_ref[:, a:b],
        w0_ref, w2_ref, a0_ref, a2_ref, wg2_ref, kk_ref, ka_ref, rk_ref, e)
    w = jnp.exp(-dexp)
    bb = kk * alr
    lane = lax.broadcasted_iota(jnp.int32, (bt, LANES), 1)

    def fill(dst, val):
        for p in range(H // 2):
            blk = val[:, p * LANES:(p + 1) * LANES]
            rolled = pltpu.roll(blk, HD, 1)
            dst[2 * p] = jnp.where(lane < HD, blk, rolled)
            dst[2 * p + 1] = jnp.where(lane < HD, rolled, blk)

    fill(a_s, -kk)
    fill(wr_s, w * r)
    fill(w_s, w)
    fill(b_s, bb)
    fill(k_s, kh)
    fill(v_s, v)
    fill(s1_s, _seg_sum(bb * r, e))
    fill(s2_s, _seg_sum(kh * r, e))

    li = lax.broadcasted_iota(jnp.int32, (LANES, LANES), 0)
    lj = lax.broadcasted_iota(jnp.int32, (LANES, LANES), 1)
    half_sum = ((li // HD) == (lj // HD)).astype(BF16)
    half_sum2 = jnp.concatenate([half_sum, half_sum], axis=0)
    src_v = 2 * (li % HD) + li // HD
    perm_lo = ((li % HD < PK) & (lj == src_v)).astype(BF16)
    perm_hi = ((li % HD < PK) & (lj == src_v + HD)).astype(BF16)
    pi = lax.broadcasted_iota(jnp.int32, (PK, LANES), 0)
    pj = lax.broadcasted_iota(jnp.int32, (PK, LANES), 1)
    pick_v = ((pj % HD) == 2 * pi + pj // HD).astype(F32)
    diag = ((pj % HD) == pi).astype(F32)

    def seq_step(b, carry):
        rb = pl.ds(b, 1)
        xa, xw, xv = [], [], []
        for h in range(H):
            s = s_ref[b, h]
            xa.append((s * a_s[h, rb, :]).astype(BF16))
            xw.append((s * wr_s[h, rb, :]).astype(BF16))
            vm = pick_v * v_s[h, rb, :]
            vhi = vm.astype(BF16)
            xv.append(jnp.concatenate([vhi, (vm - vhi.astype(F32)).astype(BF16)], axis=1))
        sa = _dot(jnp.concatenate(xa, axis=0), half_sum)
        swr = _dot(jnp.concatenate(xw, axis=0), half_sum)
        vb = _dot(jnp.concatenate(xv, axis=0), half_sum2)
        for h in range(H):
            hs = slice(h * PK, (h + 1) * PK)
            so_ref[b, h] = (s_ref[b, h] * w_s[h, rb, :] + sa[hs] * b_s[h, rb, :]
                            + vb[hs] * k_s[h, rb, :])
            y = swr[hs] + sa[hs] * s1_s[h, rb, :] + vb[hs] * s2_s[h, rb, :]
            z_s[h, rb, :] = jnp.sum(y * diag, axis=0, keepdims=True)
        return carry

    lax.fori_loop(0, bt, seq_step, 0)
    y = jnp.concatenate(
        [_dot_split(z_s[2 * p], perm_lo) + _dot_split(z_s[2 * p + 1], perm_hi) for p in range(H // 2)],
        axis=1)
    o_ref[...] = _mix_out(y, bonus, gate, sa_ref[...], mb_ref[...], x_ref[...],
                          lnw_ref, lnb_ref, wo_ref, e)


def _mix_step(xs, mb, sa, x, states, l, wts, e):
    bsz = x.shape[0]
    bt = STEP_TILE
    assert bsz % bt == 0
    state_spec = pl.BlockSpec((None, bt, H, PK, LANES), lambda i: (l, i, 0, 0, 0))

    def rows(width):
        return pl.BlockSpec((bt, width), lambda i: (i, 0))

    tokens = (xs, mb, sa, x)
    return pl.pallas_call(
        functools.partial(_mix_step_body, bt=bt),
        grid=(bsz // bt,),
        in_specs=[rows(DS), rows(D), rows(D), rows(D), state_spec]
        + [_layer_spec(w, l) for w in wts] + [_const_spec(e.shape)],
        out_specs=[rows(D), state_spec],
        out_shape=[
            jax.ShapeDtypeStruct((bsz, D), F32),
            jax.ShapeDtypeStruct(states.shape, F32),
        ],
        scratch_shapes=[pltpu.VMEM((H, bt, LANES), F32)] * 9,
        input_output_aliases={len(tokens): 1},
        compiler_params=pltpu.CompilerParams(
            dimension_semantics=("arbitrary",), vmem_limit_bytes=VMEM_LIMIT),
        name="mix_step",
    )(*tokens, states, *wts, e)


def kernel(x_prompt, x_sample, state_wkv, state_shift, state_conv, meta_tokens, norm_ffn1, ffn1_w_in, ffn1_w_out, norm_mix, w_in, mu_shift, w0, w_w2, a0, w_a2, w_g2, k_k, k_a, r_k, lnx_w, lnx_b, conv_w, w_o, norm_ffn2, ffn2_w_in, ffn2_w_out, norm_final):
    depth = norm_ffn1.shape[0]
    bp, lp, _ = x_prompt.shape
    bs = x_sample.shape[0]
    n_meta = meta_tokens.shape[0]

    def vec(a):
        return a.reshape(depth, 1, -1)

    half_zeros = jnp.zeros((depth, LORA_WA // 2, D), F32)
    wts = tuple(dict(
        w0=vec(w0),
        w2=jnp.concatenate([w_w2, half_zeros], axis=1).astype(BF16),
        a0=vec(a0),
        a2=jnp.concatenate([half_zeros, w_a2], axis=1).astype(BF16),
        wg2=w_g2.astype(BF16), k_k=vec(k_k), k_a=vec(k_a), r_k=vec(r_k), lnw=vec(lnx_w),
        lnb=vec(lnx_b), wo=w_o.astype(BF16))[name] for name in _MIX_WEIGHTS)
    lane = jnp.arange(GL)
    e = (lane[:, None] // HD == lane[None, :] // HD).astype(BF16)
    n1, nm, n2, mu = vec(norm_ffn1), vec(norm_mix), vec(norm_ffn2), vec(mu_shift)
    f1_in, f1_out = ffn1_w_in.astype(BF16), ffn1_w_out.astype(BF16)
    f2_in, f2_out = ffn2_w_in.astype(BF16), ffn2_w_out.astype(BF16)
    w_in_b = w_in.astype(BF16)
    gfin = norm_final[None]

    xm = meta_tokens.astype(x_prompt.dtype)
    xp = x_prompt.reshape(bp * lp, D)
    xs = x_sample.reshape(bs, D)
    states = state_wkv.reshape(depth, bs, H, PK, LANES)
    zero_s = jnp.zeros((1, H, HD, HD), F32)
    zero_sh = jnp.zeros((1, 1, DS), F32)
    zero_cv = jnp.zeros((1, 2, D), F32)
    p_wkv, p_shift, p_conv, s_shift, s_conv = [], [], [], [], []
    def ffn_meta_sample(xm, xs, l, g, w1, w2):
        y = _ffn(jnp.concatenate([xm, xs], axis=0), l, g, w1, w2, gfin, False)
        return y[:n_meta], y[n_meta:]

    for l in range(depth):
        xm, xs = ffn_meta_sample(xm, xs, l, n1, f1_in, f1_out)
        xp = _ffn(xp, l, n1, f1_in, f1_out, gfin, False)
        m_xs, m_mb, m_sa, m_sh, m_cv = _proj_seq(xm, n_meta, l, nm, w_in_b, mu, conv_w, zero_sh, zero_cv)
        p_xs, p_mb, p_sa, p_sh, p_cv = _proj_seq(xp, lp, l, nm, w_in_b, mu, conv_w, m_sh, m_cv)
        s_xs, s_mb, s_sa, s_sh, s_cv = _proj_step(xs, l, nm, w_in_b, mu, conv_w, state_shift, state_conv)
        xm3, m_s = _mix_seq(m_xs.reshape(1, n_meta, DS), m_mb.reshape(1, n_meta, D),
                            m_sa.reshape(1, n_meta, D), xm.reshape(1, n_meta, D), zero_s,
                            l, wts, e, tl=n_meta, c=n_meta)
        xp3, p_s = _mix_seq(p_xs.reshape(bp, lp, DS), p_mb.reshape(bp, lp, D), p_sa.reshape(bp, lp, D),
                            xp.reshape(bp, lp, D), m_s, l, wts, e, tl=min(SEQ_TILE, lp), c=CHUNK)
        xs, states = _mix_step(s_xs, s_mb, s_sa, xs, states, l, wts, e)
        xm = xm3.reshape(n_meta, D)
        xp = xp3.reshape(bp * lp, D)
        last = l == depth - 1
        if last:
            xs = _ffn(xs, l, n2, f2_in, f2_out, gfin, True)
        else:
            xm, xs = ffn_meta_sample(xm, xs, l, n2, f2_in, f2_out)
        xp = _ffn(xp, l, n2, f2_in, f2_out, gfin, last)
        p_wkv.append(p_s)
        p_shift.append(p_sh[:, 0])
        p_conv.append(p_cv)
        s_shift.append(s_sh)
        s_conv.append(s_cv)
    return (xp.reshape(bp, lp, D), xs.reshape(bs, 1, D), jnp.stack(p_wkv), jnp.stack(p_shift),
            jnp.stack(p_conv), states.reshape(state_wkv.shape), jnp.stack(s_shift), jnp.stack(s_conv))
```

```python
import functools
import math

import jax
import jax.numpy as jnp
from jax import lax
from jax.experimental import pallas as pl
from jax.experimental.pallas import tpu as pltpu

F32 = jnp.float32
BF16 = jnp.bfloat16

D = 1024
H = 16
HD = 64
G = 4
GL = G * HD
NG = H // G
LORA_WA = 128
LORA_G = 128
DS = 3 * D + LORA_WA + LORA_G
DP = DS + 3 * D + 2 * D
DFF = 2816
RMS_EPS = 1e-6
GN_EPS = 64e-5
EXP_M05 = math.exp(-0.5)
LANES = 128
MXU_DIM = 256
PK = HD * HD // LANES

CHUNK = 32
SEQ_TILE = 512
ROW_TILE = 512
PAIR_TILE = 256
PROJ_TILE = 256
STEP_TILE = 32
VMEM_LIMIT = 56 * 1024 * 1024

_NT = (((1,), (1,)), ((), ()))
_TN = (((0,), (0,)), ((), ()))

_MIX_WEIGHTS = ("w0", "w2", "a0", "a2", "wg2", "k_k", "k_a", "r_k", "lnw", "lnb", "wo")


def _dot(a, b):
    return jnp.dot(a, b, preferred_element_type=F32)


def _dot_split(x, m):
    hi = x.astype(BF16)
    lo = (x - hi.astype(F32)).astype(BF16)
    return _dot(hi, m) + _dot(lo, m)


def _rmsnorm(x, g):
    return x * lax.rsqrt(jnp.mean(x * x, axis=-1, keepdims=True) + RMS_EPS) * g


def _sigmoid(x):
    return 1.0 / (1.0 + jnp.exp(-x))


def _const_spec(shape):
    nd = len(shape)
    return pl.BlockSpec(shape, lambda *_: (0,) * nd)


def _layer_spec(arr, l, **kw):
    nd = arr.ndim - 1
    return pl.BlockSpec((None,) + arr.shape[1:], lambda *_: (l,) + (0,) * nd, **kw)


def _ffn_body(x_ref, g_ref, win_ref, wout_ref, gf_ref, o_ref, *, final_norm):
    x = x_ref[...]
    xb = _rmsnorm(x, g_ref[...]).astype(BF16)
    gate = _dot(xb, win_ref[:, 0:DFF])
    up = _dot(xb, win_ref[:, DFF:2 * DFF])
    act = (gate * _sigmoid(gate) * up).astype(BF16)
    y = x + 0.5 * _dot(act, wout_ref[...])
    if final_norm:
        y = _rmsnorm(y, gf_ref[...])
    o_ref[...] = y


def _ffn(x, l, g, w_in, w_out, g_final, final_norm):
    rows = x.shape[0]
    tm = min(ROW_TILE, rows)
    assert rows % tm == 0
    return pl.pallas_call(
        functools.partial(_ffn_body, final_norm=final_norm),
        grid=(rows // tm,),
        in_specs=[
            pl.BlockSpec((tm, D), lambda i: (i, 0)),
            _layer_spec(g, l),
            _layer_spec(w_in, l, pipeline_mode=pl.Buffered(1)),
            _layer_spec(w_out, l, pipeline_mode=pl.Buffered(1)),
            _const_spec((1, D)),
        ],
        out_specs=pl.BlockSpec((tm, D), lambda i: (i, 0)),
        out_shape=jax.ShapeDtypeStruct((rows, D), F32),
        compiler_params=pltpu.CompilerParams(
            dimension_semantics=("arbitrary",), vmem_limit_bytes=VMEM_LIMIT),
        name="ffn",
    )(x, g, w_in, w_out, g_final)


def _ffn_pair_body(x_ref, ga_ref, ain_ref, aout_ref, gb_ref, bin_ref, bout_ref, o_ref):
    x = x_ref[...]
    for g_ref, win_ref, wout_ref in ((ga_ref, ain_ref, aout_ref), (gb_ref, bin_ref, bout_ref)):
        xb = _rmsnorm(x, g_ref[...]).astype(BF16)
        gate = _dot(xb, win_ref[:, 0:DFF])
        up = _dot(xb, win_ref[:, DFF:2 * DFF])
        act = (gate * _sigmoid(gate) * up).astype(BF16)
        x = x + 0.5 * _dot(act, wout_ref[...])
    o_ref[...] = x


def _ffn_pair(x, la, ga, a_in, a_out, lb, gb, b_in, b_out):
    rows = x.shape[0]
    tm = min(PAIR_TILE, rows)
    assert rows % tm == 0

    def once(arr, l):
        return _layer_spec(arr, l, pipeline_mode=pl.Buffered(1))

    return pl.pallas_call(
        _ffn_pair_body,
        grid=(rows // tm,),
        in_specs=[
            pl.BlockSpec((tm, D), lambda i: (i, 0)),
            _layer_spec(ga, la), once(a_in, la), once(a_out, la),
            _layer_spec(gb, lb), once(b_in, lb), once(b_out, lb),
        ],
        out_specs=pl.BlockSpec((tm, D), lambda i: (i, 0)),
        out_shape=jax.ShapeDtypeStruct((rows, D), F32),
        compiler_params=pltpu.CompilerParams(
            dimension_semantics=("arbitrary",), vmem_limit_bytes=VMEM_LIMIT),
        name="ffn_pair",
    )(x, ga, a_in, a_out, gb, b_in, b_out)


def _proj_columns(x_ref, g_ref, w_ref):
    xb = _rmsnorm(x_ref[...], g_ref[...]).astype(BF16)
    ps = _dot(xb, w_ref[:, 0:DS])
    pc = _dot(xb, w_ref[:, DS:DS + 3 * D])
    pg = _dot(xb, w_ref[:, DS + 3 * D:DP])
    return ps, pc, pg


def _proj_seq_body(x_ref, g_ref, w_ref, mu_ref, cw_ref, sh0_ref, cv0_ref,
                   xs_ref, mb_ref, sa_ref, sh_ref, cv_ref, shc_s, cvc_s, *, tm, tps):
    i = pl.program_id(0)

    @pl.when(i % tps == 0)
    def _():
        shc_s[...] = sh0_ref[0]
        cvc_s[...] = cv0_ref[0]

    xb = _rmsnorm(x_ref[...], g_ref[...]).astype(BF16)
    row = lax.broadcasted_iota(jnp.int32, (tm, 1), 0)

    pc = _dot(xb, w_ref[:, DS:DS + 3 * D])
    u = pc[:, D:2 * D] * pc[:, 2 * D:3 * D]
    c0 = cvc_s[0:1, :]
    c1 = cvc_s[1:2, :]
    u1 = jnp.where(row == 0, c1, pltpu.roll(u, 1, 0))
    u2 = jnp.where(row == 0, c0, jnp.where(row == 1, c1, pltpu.roll(u, 2, 0)))
    z = cw_ref[0:1, :] * u2 + cw_ref[1:2, :] * u1 + cw_ref[2:3, :] * u
    pg = _dot(xb, w_ref[:, DS + 3 * D:DP])
    mb_ref[...] = _sigmoid(pg[:, D:2 * D]) * (pc[:, 0:D] * z)
    sa_ref[...] = _sigmoid(pg[:, 0:D])
    last_u = u[tm - 2:tm, :]
    cvc_s[...] = last_u

    ps = _dot(xb, w_ref[:, 0:DS])
    prev = jnp.where(row == 0, shc_s[...], pltpu.roll(ps, 1, 0))
    xs_ref[...] = ps + (prev - ps) * mu_ref[...]
    last_ps = ps[tm - 1:tm, :]
    shc_s[...] = last_ps

    @pl.when(i % tps == tps - 1)
    def _():
        sh_ref[0] = last_ps
        cv_ref[0] = last_u


def _proj_seq(x, seq_len, l, g, w, mu, cw, sh0, cv0):
    rows = x.shape[0]
    tm = min(PROJ_TILE, seq_len)
    assert rows % seq_len == 0 and seq_len % tm == 0 and tm >= 2
    nseq, tps = rows // seq_len, seq_len // tm
    return pl.pallas_call(
        functools.partial(_proj_seq_body, tm=tm, tps=tps),
        grid=(rows // tm,),
        in_specs=[
            pl.BlockSpec((tm, D), lambda i: (i, 0)),
            _layer_spec(g, l),
            _layer_spec(w, l, pipeline_mode=pl.Buffered(1)),
            _layer_spec(mu, l),
            _layer_spec(cw, l),
            _const_spec((1, 1, DS)),
            _const_spec((1, 2, D)),
        ],
        out_specs=[
            pl.BlockSpec((tm, DS), lambda i: (i, 0)),
            pl.BlockSpec((tm, D), lambda i: (i, 0)),
            pl.BlockSpec((tm, D), lambda i: (i, 0)),
            pl.BlockSpec((1, 1, DS), lambda i: (i // tps, 0, 0)),
            pl.BlockSpec((1, 2, D), lambda i: (i // tps, 0, 0)),
        ],
        out_shape=[
            jax.ShapeDtypeStruct((rows, DS), F32),
            jax.ShapeDtypeStruct((rows, D), F32),
            jax.ShapeDtypeStruct((rows, D), F32),
            jax.ShapeDtypeStruct((nseq, 1, DS), F32),
            jax.ShapeDtypeStruct((nseq, 2, D), F32),
        ],
        scratch_shapes=[pltpu.VMEM((1, DS), F32), pltpu.VMEM((2, D), F32)],
        compiler_params=pltpu.CompilerParams(
            dimension_semantics=("arbitrary",), vmem_limit_bytes=VMEM_LIMIT),
        name="proj_seq",
    )(x, g, w, mu, cw, sh0, cv0)


def _proj_step_body(x_ref, g_ref, w_ref, mu_ref, cw_ref, sh_ref, cv_ref,
                    xs_ref, mb_ref, sa_ref, sho_ref, cvo_ref):
    ps, pc, pg = _proj_columns(x_ref, g_ref, w_ref)
    xs_ref[...] = ps + (sh_ref[...] - ps) * mu_ref[...]
    sho_ref[...] = ps
    u = pc[:, D:2 * D] * pc[:, 2 * D:3 * D]
    c1 = cv_ref[:, 1, :]
    z = cw_ref[0:1, :] * cv_ref[:, 0, :] + cw_ref[1:2, :] * c1 + cw_ref[2:3, :] * u
    mb_ref[...] = _sigmoid(pg[:, D:2 * D]) * (pc[:, 0:D] * z)
    sa_ref[...] = _sigmoid(pg[:, 0:D])
    cvo_ref[:, 0, :] = c1
    cvo_ref[:, 1, :] = u


def _proj_step(x, l, g, w, mu, cw, sh_all, cv_all):
    rows = x.shape[0]
    return pl.pallas_call(
        _proj_step_body,
        grid=(1,),
        in_specs=[
            _const_spec((rows, D)),
            _layer_spec(g, l),
            _layer_spec(w, l, pipeline_mode=pl.Buffered(1)),
            _layer_spec(mu, l),
            _layer_spec(cw, l),
            _layer_spec(sh_all, l),
            _layer_spec(cv_all, l),
        ],
        out_specs=[
            _const_spec((rows, DS)),
            _const_spec((rows, D)),
            _const_spec((rows, D)),
            _const_spec((rows, DS)),
            _const_spec((rows, 2, D)),
        ],
        out_shape=[
            jax.ShapeDtypeStruct((rows, DS), F32),
            jax.ShapeDtypeStruct((rows, D), F32),
            jax.ShapeDtypeStruct((rows, D), F32),
            jax.ShapeDtypeStruct((rows, DS), F32),
            jax.ShapeDtypeStruct((rows, 2, D), F32),
        ],
        compiler_params=pltpu.CompilerParams(
            dimension_semantics=("arbitrary",), vmem_limit_bytes=VMEM_LIMIT),
        name="proj_step",
    )(x, g, w, mu, cw, sh_all, cv_all)


def _seg_sum(x, e):
    return jnp.concatenate([_dot(x[:, g * GL:(g + 1) * GL].astype(BF16), e) for g in range(NG)], axis=1)


def _mix_heads(xs, w0_ref, w2_ref, a0_ref, a2_ref, wg2_ref, kk_ref, ka_ref, rk_ref, e):
    r = xs(0, D)
    k = xs(D, 2 * D)
    v = xs(2 * D, 3 * D)
    lwla = xs(3 * D, 3 * D + LORA_WA)
    lg = xs(3 * D + LORA_WA, DS)
    wl = w0_ref[...] + _dot(jnp.tanh(lwla).astype(BF16), w2_ref[...])
    dexp = EXP_M05 * _sigmoid(wl)
    alr = _sigmoid(a0_ref[...] + _dot(lwla.astype(BF16), a2_ref[...]))
    gate = _dot(_sigmoid(lg).astype(BF16), wg2_ref[...])
    kkr = k * kk_ref[...]
    kk = kkr * lax.rsqrt(jnp.maximum(_seg_sum(kkr * kkr, e), 1e-24))
    kh = k * (1.0 + (alr - 1.0) * ka_ref[...])
    bonus = _seg_sum(r * kh * rk_ref[...], e) * v
    return r, kh, v, kk, alr, dexp, gate, bonus


def _mix_out(y, bonus, gate, sa, mb, x, lnw_ref, lnb_ref, wo_ref, e):
    mu = _seg_sum(y, e) * (1.0 / HD)
    d = y - mu
    var = _seg_sum(d * d, e) * (1.0 / HD)
    yn = d * lax.rsqrt(var + GN_EPS) * lnw_ref[...] + lnb_ref[...]
    m = sa * ((yn + bonus) * gate) + mb
    return x + _dot(m.astype(BF16), wo_ref[...])


def _mix_seq_body(xs_ref, mb_ref, sa_ref, x_ref, s0_ref,
                  w0_ref, w2_ref, a0_ref, a2_ref, wg2_ref, kk_ref, ka_ref, rk_ref,
                  lnw_ref, lnb_ref, wo_ref, e_ref,
                  o_ref, so_ref,
                  st_s, ah_s, rh_s, bt_s, kt_s, v_s, wl_s, y_s, *, tl, c):
    j = pl.program_id(1)
    gc = G * c
    nchunk = tl // c

    @pl.when(j == 0)
    def _():
        st_s[...] = jnp.zeros_like(st_s)
        for h in range(H):
            d0 = (h % G) * HD
            st_s[h // G, d0:d0 + HD, d0:d0 + HD] = s0_ref[0, h]

    e = e_ref[...]
    row = lax.broadcasted_iota(jnp.int32, (tl, 1), 0)
    r, kh, v, kk, alr, dexp, gate, bonus = _mix_heads(
        lambda a, b: xs_ref[0, :, a:b],
        w0_ref, w2_ref, a0_ref, a2_ref, wg2_ref, kk_ref, ka_ref, rk_ref, e)

    rin = row % c
    cum = dexp
    s = 1
    while s < c:
        cum = cum + jnp.where(rin >= s, pltpu.roll(cum, s, 0), 0.0)
        s *= 2
    wc = jnp.exp(-cum)
    winv = jnp.exp(cum)
    rh_s[...] = (r * wc).astype(BF16)
    ah_s[...] = (-kk * jnp.exp(dexp - cum)).astype(BF16)
    bt_s[...] = (kk * alr * winv).astype(BF16)
    kt_s[...] = (kh * winv).astype(BF16)
    v_s[...] = v.astype(BF16)
    for ci in range(nchunk):
        wl_s[ci:ci + 1, :] = wc[(ci + 1) * c - 1:(ci + 1) * c, :]

    groups = range(NG)
    lanes = [slice(g * GL, (g + 1) * GL) for g in groups]

    ri = lax.broadcasted_iota(jnp.int32, (gc, gc), 0)
    cj = lax.broadcasted_iota(jnp.int32, (gc, gc), 1)
    same = (ri // c) == (cj // c)
    strict = same & ((ri % c) > (cj % c))
    incl = same & ((ri % c) >= (cj % c))
    eye = (ri == cj).astype(F32)
    smask = (lax.broadcasted_iota(jnp.int32, (gc, GL), 0) // c) == (
        lax.broadcasted_iota(jnp.int32, (gc, GL), 1) // HD)

    def stack(x):
        return jnp.where(smask, jnp.concatenate([x] * G, axis=0), jnp.zeros((), x.dtype))

    def nt(a, b):
        return lax.dot_general(a, b, _NT, preferred_element_type=F32)

    def tn(a, b):
        return lax.dot_general(a, b, _TN, preferred_element_type=F32)

    merged = gc % LANES == 0 and 2 * gc <= MXU_DIM

    def state_free_stages(q):
        rs = slice(q * c, (q + 1) * c)
        t = {"rs": rs}

        def s_products_a():
            t["ah"] = [ah_s[rs, ls] for ls in lanes]
            t["bt"] = [bt_s[rs, ls] for ls in lanes]
            t["kt"] = [kt_s[rs, ls] for ls in lanes]
            a_st = [stack(x) for x in t["ah"]]
            b_rep = [jnp.concatenate([x] * G, axis=0) for x in t["bt"]]
            k_rep = [jnp.concatenate([x] * G, axis=0) for x in t["kt"]]
            if merged:
                t["rh"] = [rh_s[rs, ls] for ls in lanes]
                ar_st = [jnp.concatenate([a_st[g], stack(t["rh"][g])], axis=0) for g in groups]
                bk = [jnp.concatenate([b_rep[g], k_rep[g]], axis=0) for g in groups]
                gar = [nt(ar_st[g], bk[g]) for g in groups]
                l_ab = [jnp.where(strict, gar[g][:gc, :gc], 0.0) for g in groups]
                t["a_ak"] = [jnp.where(strict, gar[g][:gc, gc:], 0.0).astype(BF16) for g in groups]
                t["a_r"] = [jnp.concatenate([jnp.where(incl, gar[g][gc:, :gc], 0.0),
                                             jnp.where(incl, gar[g][gc:, gc:], 0.0)], axis=1).astype(BF16)
                            for g in groups]
            else:
                t["b_rep"], t["k_rep"] = b_rep, k_rep
                l_ab = [jnp.where(strict, nt(a_st[g], b_rep[g]), 0.0) for g in groups]
                t["a_ak"] = [jnp.where(strict, nt(a_st[g], k_rep[g]), 0.0).astype(BF16) for g in groups]
            t["lp"] = [x.astype(BF16) for x in l_ab]
            t["p"] = [eye + x for x in l_ab]

        def s_products_r():
            if not merged:
                t["rh"] = [rh_s[rs, ls] for ls in lanes]
                r_st = [stack(x) for x in t["rh"]]
                t["a_rb"] = [jnp.where(incl, nt(r_st[g], t["b_rep"][g]), 0.0).astype(BF16) for g in groups]
                t["a_rk"] = [jnp.where(incl, nt(r_st[g], t["k_rep"][g]), 0.0).astype(BF16) for g in groups]

        def s_av():
            t["v_st"] = [stack(v_s[rs, ls]) for ls in lanes]
            t["av"] = [_dot(t["a_ak"][g], t["v_st"][g]) for g in groups]

        def s_square():
            t["lp"] = [_dot(x, x).astype(BF16) for x in t["lp"]]

        def s_extend():
            t["p"] = [t["p"][g] + _dot(t["lp"][g], t["p"][g].astype(BF16)) for g in groups]

        def s_square_extend():
            both = [_dot(t["lp"][g], jnp.concatenate([t["lp"][g], t["p"][g].astype(BF16)], axis=1))
                    for g in groups]
            t["lp"] = [x[:, :gc].astype(BF16) for x in both]
            t["p"] = [t["p"][g] + both[g][:, gc:] for g in groups]

        def s_finish():
            t["pb"] = [x.astype(BF16) for x in t["p"]]
            t["b_st"] = [stack(x) for x in t["bt"]]
            t["k_st"] = [stack(x) for x in t["kt"]]

        stages = [s_products_a, s_products_r]
        factors = c.bit_length() - 1
        if merged:
            stages += [s_square] + [s_square_extend] * (factors - 2) + [s_extend]
        else:
            stages += [s_square, s_extend] * (factors - 1)
        return t, stages + [s_av, s_finish]

    def state_chain_stages(q, t, st):
        rs = t["rs"]
        w = {}

        def s_read_state():
            w["ars"] = [nt(jnp.concatenate([t["ah"][g], t["rh"][g]], axis=0), st[g].astype(BF16))
                        for g in groups]

        def s_solve():
            x_st = [stack(w["ars"][g][:c]) + t["av"][g] for g in groups]
            w["u"] = [_dot(t["pb"][g], x_st[g].astype(BF16)).astype(BF16) for g in groups]

        def s_outputs():
            if merged:
                uv = [jnp.concatenate([w["u"][g], t["v_st"][g]], axis=0) for g in groups]
                w["y"] = [_dot(t["a_r"][g], uv[g]) for g in groups]
                w["upd"] = [tn(uv[g], jnp.concatenate([t["b_st"][g], t["k_st"][g]], axis=0))
                            for g in groups]
            else:
                w["y"] = [_dot(t["a_rb"][g], w["u"][g]) + _dot(t["a_rk"][g], t["v_st"][g]) for g in groups]
                w["upd"] = [tn(w["u"][g], t["b_st"][g]) + tn(t["v_st"][g], t["k_st"][g]) for g in groups]

        def s_commit():
            for g in groups:
                y = w["ars"][g][c:]
                for hh in range(G):
                    y = y + w["y"][g][hh * c:(hh + 1) * c]
                y_s[rs, lanes[g]] = y
                st[g] = (st[g] + w["upd"][g]) * wl_s[q:q + 1, lanes[g]]

        return [s_read_state, s_solve, s_outputs, s_commit]

    st = [st_s[g] for g in groups]
    cur_t, first = state_free_stages(0)
    for stage in first:
        stage()
    for q in range(nchunk):
        chain = state_chain_stages(q, cur_t, st)
        nxt_t, free = state_free_stages(q + 1) if q + 1 < nchunk else (None, [])
        every = max(1, len(free) // len(chain))
        fi = 0
        for stage in chain:
            for _ in range(every):
                if fi < len(free):
                    free[fi]()
                    fi += 1
            stage()
        while fi < len(free):
            free[fi]()
            fi += 1
        cur_t = nxt_t
    for g in groups:
        st_s[g] = st[g]

    o_ref[0] = _mix_out(y_s[...], bonus, gate, sa_ref[0], mb_ref[0], x_ref[0],
                        lnw_ref, lnb_ref, wo_ref, e)

    @pl.when(j == pl.num_programs(1) - 1)
    def _():
        for h in range(H):
            d0 = (h % G) * HD
            so_ref[0, h] = st_s[h // G, d0:d0 + HD, d0:d0 + HD]


def _mix_seq(xs, mb, sa, x, s0, l, wts, e, *, tl, c):
    bsz, length, _ = x.shape
    assert length % tl == 0 and tl % c == 0
    nchunk = tl // c

    def tile(width):
        return pl.BlockSpec((1, tl, width), lambda b, j: (b, j, 0))

    return pl.pallas_call(
        functools.partial(_mix_seq_body, tl=tl, c=c),
        grid=(bsz, length // tl),
        in_specs=[tile(DS), tile(D), tile(D), tile(D), _const_spec((1, H, HD, HD))]
        + [_layer_spec(w, l) for w in wts] + [_const_spec(e.shape)],
        out_specs=[tile(D), pl.BlockSpec((1, H, HD, HD), lambda b, j: (b, 0, 0, 0))],
        out_shape=[
            jax.ShapeDtypeStruct((bsz, length, D), F32),
            jax.ShapeDtypeStruct((bsz, H, HD, HD), F32),
        ],
        scratch_shapes=[pltpu.VMEM((NG, GL, GL), F32)] + [pltpu.VMEM((tl, D), BF16)] * 5 + [
            pltpu.VMEM((max(8, nchunk), D), F32),
            pltpu.VMEM((tl, D), F32),
        ],
        compiler_params=pltpu.CompilerParams(
            dimension_semantics=("arbitrary", "arbitrary"), vmem_limit_bytes=VMEM_LIMIT),
        name=f"mix_seq_{tl}",
    )(xs, mb, sa, x, s0, *wts, e)


def _mix_step_body(xs_ref, mb_ref, sa_ref, x_ref, s_ref,
                   w0_ref, w2_ref, a0_ref, a2_ref, wg2_ref, kk_ref, ka_ref, rk_ref,
                   lnw_ref, lnb_ref, wo_ref, e_ref,
                   o_ref, so_ref,
                   a_s, wr_s, w_s, b_s, k_s, v_s, s1_s, s2_s, z_s, *, bt):
    e = e_ref[...]
    r, kh, v, kk, alr, dexp, gate, bonus = _mix_heads(
        lambda a, b: xs_ref[:, a:b],
        w0_ref, w2_ref, a0_ref, a2_ref, wg2_ref, kk_ref, ka_ref, rk_ref, e)
    w = jnp.exp(-dexp)
    bb = kk * alr
    lane = lax.broadcasted_iota(jnp.int32, (bt, LANES), 1)

    def fill(dst, val):
        for p in range(H // 2):
            blk = val[:, p * LANES:(p + 1) * LANES]
            rolled = pltpu.roll(blk, HD, 1)
            dst[2 * p] = jnp.where(lane < HD, blk, rolled)
            dst[2 * p + 1] = jnp.where(lane < HD, rolled, blk)

    fill(a_s, -kk)
    fill(wr_s, w * r)
    fill(w_s, w)
    fill(b_s, bb)
    fill(k_s, kh)
    fill(v_s, v)
    fill(s1_s, _seg_sum(bb * r, e))
    fill(s2_s, _seg_sum(kh * r, e))

    li = lax.broadcasted_iota(jnp.int32, (LANES, LANES), 0)
    lj = lax.broadcasted_iota(jnp.int32, (LANES, LANES), 1)
    half_sum = ((li // HD) == (lj // HD)).astype(BF16)
    half_sum2 = jnp.concatenate([half_sum, half_sum], axis=0)
    src_v = 2 * (li % HD) + li // HD
    perm_lo = ((li % HD < PK) & (lj == src_v)).astype(BF16)
    perm_hi = ((li % HD < PK) & (lj == src_v + HD)).astype(BF16)
    pi = lax.broadcasted_iota(jnp.int32, (PK, LANES), 0)
    pj = lax.broadcasted_iota(jnp.int32, (PK, LANES), 1)
    pick_v = ((pj % HD) == 2 * pi + pj // HD).astype(F32)
    diag = ((pj % HD) == pi).astype(F32)

    def seq_step(b, carry):
        rb = pl.ds(b, 1)
        xa, xw, xv = [], [], []
        for h in range(H):
            s = s_ref[b, h]
            xa.append((s * a_s[h, rb, :]).astype(BF16))
            xw.append((s * wr_s[h, rb, :]).astype(BF16))
            vm = pick_v * v_s[h, rb, :]
            vhi = vm.astype(BF16)
            xv.append(jnp.concatenate([vhi, (vm - vhi.astype(F32)).astype(BF16)], axis=1))
        sa = _dot(jnp.concatenate(xa, axis=0), half_sum)
        swr = _dot(jnp.concatenate(xw, axis=0), half_sum)
        vb = _dot(jnp.concatenate(xv, axis=0), half_sum2)
        for h in range(H):
            hs = slice(h * PK, (h + 1) * PK)
            so_ref[b, h] = (s_ref[b, h] * w_s[h, rb, :] + sa[hs] * b_s[h, rb, :]
                            + vb[hs] * k_s[h, rb, :])
            y = swr[hs] + sa[hs] * s1_s[h, rb, :] + vb[hs] * s2_s[h, rb, :]
            z_s[h, rb, :] = jnp.sum(y * diag, axis=0, keepdims=True)
        return carry

    lax.fori_loop(0, bt, seq_step, 0)
    y = jnp.concatenate(
        [_dot_split(z_s[2 * p], perm_lo) + _dot_split(z_s[2 * p + 1], perm_hi) for p in range(H // 2)],
        axis=1)
    o_ref[...] = _mix_out(y, bonus, gate, sa_ref[...], mb_ref[...], x_ref[...],
                          lnw_ref, lnb_ref, wo_ref, e)


def _mix_step(xs, mb, sa, x, states, l, wts, e):
    bsz = x.shape[0]
    bt = STEP_TILE
    assert bsz % bt == 0
    state_spec = pl.BlockSpec((None, bt, H, PK, LANES), lambda i: (l, i, 0, 0, 0))

    def rows(width):
        return pl.BlockSpec((bt, width), lambda i: (i, 0))

    tokens = (xs, mb, sa, x)
    return pl.pallas_call(
        functools.partial(_mix_step_body, bt=bt),
        grid=(bsz // bt,),
        in_specs=[rows(DS), rows(D), rows(D), rows(D), state_spec]
        + [_layer_spec(w, l) for w in wts] + [_const_spec(e.shape)],
        out_specs=[rows(D), state_spec],
        out_shape=[
            jax.ShapeDtypeStruct((bsz, D), F32),
            jax.ShapeDtypeStruct(states.shape, F32),
        ],
        scratch_shapes=[pltpu.VMEM((H, bt, LANES), F32)] * 9,
        input_output_aliases={len(tokens): 1},
        compiler_params=pltpu.CompilerParams(
            dimension_semantics=("arbitrary",), vmem_limit_bytes=VMEM_LIMIT),
        name="mix_step",
    )(*tokens, states, *wts, e)


def kernel(x_prompt, x_sample, state_wkv, state_shift, state_conv, meta_tokens, norm_ffn1, ffn1_w_in, ffn1_w_out, norm_mix, w_in, mu_shift, w0, w_w2, a0, w_a2, w_g2, k_k, k_a, r_k, lnx_w, lnx_b, conv_w, w_o, norm_ffn2, ffn2_w_in, ffn2_w_out, norm_final):
    depth = norm_ffn1.shape[0]
    bp, lp, _ = x_prompt.shape
    bs = x_sample.shape[0]
    n_meta = meta_tokens.shape[0]

    def vec(a):
        return a.reshape(depth, 1, -1)

    half_zeros = jnp.zeros((depth, LORA_WA // 2, D), F32)
    wts = tuple(dict(
        w0=vec(w0),
        w2=jnp.concatenate([w_w2, half_zeros], axis=1).astype(BF16),
        a0=vec(a0),
        a2=jnp.concatenate([half_zeros, w_a2], axis=1).astype(BF16),
        wg2=w_g2.astype(BF16), k_k=vec(k_k), k_a=vec(k_a), r_k=vec(r_k), lnw=vec(lnx_w),
        lnb=vec(lnx_b), wo=w_o.astype(BF16))[name] for name in _MIX_WEIGHTS)
    lane = jnp.arange(GL)
    e = (lane[:, None] // HD == lane[None, :] // HD).astype(BF16)
    n1, nm, n2, mu = vec(norm_ffn1), vec(norm_mix), vec(norm_ffn2), vec(mu_shift)
    f1_in, f1_out = ffn1_w_in.astype(BF16), ffn1_w_out.astype(BF16)
    f2_in, f2_out = ffn2_w_in.astype(BF16), ffn2_w_out.astype(BF16)
    w_in_b = w_in.astype(BF16)
    gfin = norm_final[None]

    xm = meta_tokens.astype(x_prompt.dtype)
    xp = x_prompt.reshape(bp * lp, D)
    xs = x_sample.reshape(bs, D)
    states = state_wkv.reshape(depth, bs, H, PK, LANES)
    zero_s = jnp.zeros((1, H, HD, HD), F32)
    zero_sh = jnp.zeros((1, 1, DS), F32)
    zero_cv = jnp.zeros((1, 2, D), F32)
    p_wkv, p_shift, p_conv, s_shift, s_conv = [], [], [], [], []
    def ffn_meta_sample(xm, xs, l, g, w1, w2):
        y = _ffn(jnp.concatenate([xm, xs], axis=0), l, g, w1, w2, gfin, False)
        return y[:n_meta], y[n_meta:]

    for l in range(depth):
        if l == 0:
            xm, xs = ffn_meta_sample(xm, xs, l, n1, f1_in, f1_out)
            xp = _ffn(xp, l, n1, f1_in, f1_out, gfin, False)
        else:
            y = _ffn_pair(jnp.concatenate([xm, xs], axis=0), l - 1, n2, f2_in, f2_out, l, n1, f1_in, f1_out)
            xm, xs = y[:n_meta], y[n_meta:]
            xp = _ffn_pair(xp, l - 1, n2, f2_in, f2_out, l, n1, f1_in, f1_out)
        m_xs, m_mb, m_sa, m_sh, m_cv = _proj_seq(xm, n_meta, l, nm, w_in_b, mu, conv_w, zero_sh, zero_cv)
        p_xs, p_mb, p_sa, p_sh, p_cv = _proj_seq(xp, lp, l, nm, w_in_b, mu, conv_w, m_sh, m_cv)
        s_xs, s_mb, s_sa, s_sh, s_cv = _proj_step(xs, l, nm, w_in_b, mu, conv_w, state_shift, state_conv)
        xm3, m_s = _mix_seq(m_xs.reshape(1, n_meta, DS), m_mb.reshape(1, n_meta, D),
                            m_sa.reshape(1, n_meta, D), xm.reshape(1, n_meta, D), zero_s,
                            l, wts, e, tl=n_meta, c=n_meta)
        xp3, p_s = _mix_seq(p_xs.reshape(bp, lp, DS), p_mb.reshape(bp, lp, D), p_sa.reshape(bp, lp, D),
                            xp.reshape(bp, lp, D), m_s, l, wts, e, tl=min(SEQ_TILE, lp), c=CHUNK)
        xs, states = _mix_step(s_xs, s_mb, s_sa, xs, states, l, wts, e)
        xm = xm3.reshape(n_meta, D)
        xp = xp3.reshape(bp * lp, D)
        last = l == depth - 1
        if last:
            xs = _ffn(xs, l, n2, f2_in, f2_out, gfin, True)
            xp = _ffn(xp, l, n2, f2_in, f2_out, gfin, True)
        p_wkv.append(p_s)
        p_shift.append(p_sh[:, 0])
        p_conv.append(p_cv)
        s_shift.append(s_sh)
        s_conv.append(s_cv)
    return (xp.reshape(bp, lp, D), xs.reshape(bs, 1, D), jnp.stack(p_wkv), jnp.stack(p_shift),
            jnp.stack(p_conv), states.reshape(state_wkv.shape), jnp.stack(s_shift), jnp.stack(s_conv))
```
